```python
import math
import jax
import jax.numpy as jnp
from jax import lax
import numpy as np

D_MODEL = 1024
BATCH = 2
SEQ = 8192
DEPTH = 2
DEC_BATCH = 32
DEC_SEQ = 8
PAST_LEN = 16384
PAGE_SIZE = 128

HEAD_DIM = 64
MIX_WIDTH = D_MODEL
EPS = 1e-6
A_WIDTH = MIX_WIDTH // 4
A_BLOCKS = A_WIDTH // HEAD_DIM
A_BLOCK = A_WIDTH // A_BLOCKS
CONV_W = 4
RG_C = 8.0
B_HEADS = (MIX_WIDTH // 2) // HEAD_DIM
B_KV_HEADS = B_HEADS // 2
B_GROUP = B_HEADS // B_KV_HEADS
B_WIDTH = B_HEADS * HEAD_DIM
ROPE_DIM = HEAD_DIM // 4
ROPE_THETA = 500000.0
IDX_HEADS = 4
IDX_DIM = 64
TOPK_MAX = 256
Q_BLOCK = 128
C_WIDTH = MIX_WIDTH - A_WIDTH - B_WIDTH
C_HEADS = 4
C_DV = C_WIDTH // C_HEADS
C_DK = C_DV // 2
GATE_RANK = 16
GATE_TAU = 16.0
CHUNK = 64
D_FF = 4 * D_MODEL

IN_SPLITS = (A_WIDTH, A_WIDTH,
             B_HEADS * HEAD_DIM, B_KV_HEADS * HEAD_DIM, B_KV_HEADS * HEAD_DIM,
             IDX_HEADS * IDX_DIM, IDX_DIM, IDX_HEADS,
             C_HEADS * C_DK, C_HEADS * C_DK, C_WIDTH, C_WIDTH, GATE_RANK)
N_IN = sum(IN_SPLITS)

kernel_name = "hybrid_rglru_dsa_gla_step"


def rmsnorm(x, g):
    xf = x.astype(jnp.float32)
    y = xf * lax.rsqrt(jnp.mean(xf * xf, axis=-1, keepdims=True) + EPS)
    return (y * g.astype(jnp.float32)).astype(x.dtype)


def split_cols(h):
    outs, off = [], 0
    for n in IN_SPLITS:
        outs.append(h[..., off:off + n])
        off += n
    return outs


def partial_rope(x, pos):
    half = ROPE_DIM // 2
    inv_freq = ROPE_THETA ** (-(jnp.arange(half, dtype=jnp.float32) * 2.0 / ROPE_DIM))
    ang = pos.astype(jnp.float32)[:, None] * inv_freq[None, :]
    cos = jnp.cos(ang)[:, None, :]
    sin = jnp.sin(ang)[:, None, :]
    xr = x[..., :ROPE_DIM].astype(jnp.float32)
    x1, x2 = xr[..., :half], xr[..., half:]
    rot = jnp.concatenate([x1 * cos - x2 * sin, x2 * cos + x1 * sin], axis=-1).astype(x.dtype)
    return jnp.concatenate([rot, x[..., ROPE_DIM:]], axis=-1)


def linear_scan(a, b, h0):
    def combine(l, r):
        al, bl = l
        ar, br = r
        return al * ar, ar * bl + br
    a_cum, b_cum = lax.associative_scan(combine, (a, b), axis=1)
    return a_cum * h0[:, None, :] + b_cum


def rglru_mixer(a_gate, a_x, conv_state, h0, conv_w, conv_b, rg_wa, rg_ba, rg_wx, rg_bx, rg_lambda):
    f32 = jnp.float32
    Bn, T, W = a_x.shape
    xc = jnp.concatenate([conv_state.astype(a_x.dtype), a_x], axis=1)
    u = conv_b
    for j in range(CONV_W):
        u = u + xc[:, j:j + T] * conv_w[j]
    ub = u.reshape(Bn, T, A_BLOCKS, A_BLOCK)
    r = jax.nn.sigmoid((jnp.einsum('btnd,nde->btne', ub, rg_wa).reshape(Bn, T, W) + rg_ba).astype(f32))
    i = jax.nn.sigmoid((jnp.einsum('btnd,nde->btne', ub, rg_wx).reshape(Bn, T, W) + rg_bx).astype(f32))
    log_a = -RG_C * r * jax.nn.softplus(-rg_lambda.astype(f32))
    a = jnp.exp(log_a)
    b = jnp.sqrt(-jnp.expm1(2.0 * log_a)) * i * u.astype(f32)
    h = linear_scan(a, b, h0.astype(f32))
    y = (jax.nn.gelu(a_gate.astype(f32)) * h).astype(a_x.dtype)
    return y, xc[:, T:], h[:, -1].astype(a_x.dtype)


def gla_chunk(S0, q, k, v, lg):
    cn = q.shape[2]
    b = jnp.cumsum(lg, axis=2)
    causal = jnp.tril(jnp.ones((cn, cn), dtype=bool))
    diff = b[:, :, :, None, :] - b[:, :, None, :, :]
    decay = jnp.exp(jnp.where(causal[:, :, None], diff, -jnp.inf))
    attn = jnp.einsum('bhtd,bhsd,bhtsd->bhts', q, k, decay)
    o = jnp.einsum('bhts,bhsv->bhtv', attn, v) + jnp.einsum('bhtd,bhdv->bhtv', q * jnp.exp(b), S0)
    b_last = b[:, :, -1:, :]
    S = jnp.exp(b_last[:, :, 0, :])[..., None] * S0 + jnp.einsum('bhsd,bhsv->bhdv', k * jnp.exp(b_last - b), v)
    return S, o


def gla_mixer(c_q, c_k, c_v, c_g, c_a, S0, gla_wa2, gla_ba, gla_norm):
    f32 = jnp.float32
    Bn, T, _ = c_q.shape
    def heads(t, d):
        return t.reshape(Bn, T, C_HEADS, d).transpose(0, 2, 1, 3).astype(f32)
    q = heads(c_q, C_DK) * (C_DK ** -0.5)
    k = heads(c_k, C_DK)
    v = heads(c_v, C_DV)
    lg = heads(jax.nn.log_sigmoid((c_a @ gla_wa2 + gla_ba).astype(f32)) / GATE_TAU, C_DK)
    ch = math.gcd(T, CHUNK)
    nc = T // ch
    def to_chunks(t):
        return jnp.moveaxis(t.reshape(Bn, C_HEADS, nc, ch, t.shape[-1]), 2, 0)
    S, o = lax.scan(lambda S, inp: gla_chunk(S, *inp), S0.astype(f32),
                    (to_chunks(q), to_chunks(k), to_chunks(v), to_chunks(lg)))
    o = jnp.moveaxis(o, 0, 2).reshape(Bn, C_HEADS, T, C_DV).transpose(0, 2, 1, 3)
    o = rmsnorm(o, gla_norm)
    y = (o.reshape(Bn, T, C_WIDTH) * jax.nn.silu(c_g.astype(f32))).astype(c_q.dtype)
    return y, S.astype(c_q.dtype)


def index_scores(qi, wi, ki):
    dots = jnp.einsum('bqhd,bsd->bqhs', qi, ki) * (IDX_DIM ** -0.5)
    return jnp.einsum('bqh,bqhs->bqs', wi, jax.nn.relu(dots)).astype(jnp.float32)


def sparse_attend(q, kg, vg, valid):
    Bn, Q = q.shape[0], q.shape[1]
    qg = q.reshape(Bn, Q, B_KV_HEADS, B_GROUP, HEAD_DIM)
    logits = jnp.einsum('bqgrd,bqkgd->bqgrk', qg, kg).astype(jnp.float32) * (HEAD_DIM ** -0.5)
    logits = jnp.where(valid[:, :, None, None, :], logits, -jnp.inf)
    p = jax.nn.softmax(logits, axis=-1).astype(vg.dtype)
    o = jnp.einsum('bqgrk,bqkgd->bqgrd', p, vg)
    return o.reshape(Bn, Q, B_HEADS * HEAD_DIM)


def dsa_prompt(q, k, v, qi, ki, wi):
    Bn, T = q.shape[0], q.shape[1]
    topk = min(TOPK_MAX, T // 4)
    qb = math.gcd(T, Q_BLOCK)
    nb = T // qb
    kpos = jnp.arange(T)
    take = jax.vmap(lambda rows, idx: rows[idx])
    def blocks(t):
        return jnp.moveaxis(t.reshape((Bn, nb, qb) + t.shape[2:]), 1, 0)
    def one_block(args):
        q_b, qi_b, wi_b, start = args
        qpos = start + jnp.arange(qb)
        score = index_scores(qi_b, wi_b, ki)
        score = jnp.where(kpos[None, None, :] <= qpos[None, :, None], score, -jnp.inf)
        _, top_i = lax.top_k(score, topk)
        valid = top_i <= qpos[None, :, None]
        return sparse_attend(q_b, take(k, top_i), take(v, top_i), valid)
    out = lax.map(one_block, (blocks(q), blocks(qi), blocks(wi), jnp.arange(nb) * qb))
    return jnp.moveaxis(out, 0, 1).reshape(Bn, T, B_WIDTH)


def dsa_sample(q, k, v, qi, ki, wi, cache_k, cache_v, cache_ki, page_table):
    Bn, T = q.shape[0], q.shape[1]
    n_pages = PAST_LEN // PAGE_SIZE
    past = n_pages * PAGE_SIZE
    L = past + T
    topk = min(TOPK_MAX, L // 4)
    take = jax.vmap(lambda rows, idx: rows[idx])
    ki_past = cache_ki[page_table].reshape(Bn, past, IDX_DIM).astype(ki.dtype)
    ki_all = jnp.concatenate([ki_past, ki], axis=1)
    qpos = past + jnp.arange(T)
    kpos = jnp.arange(L)
    score = index_scores(qi, wi, ki_all)
    score = jnp.where(kpos[None, None, :] <= qpos[None, :, None], score, -jnp.inf)
    _, top_i = lax.top_k(score, topk)
    valid = top_i <= qpos[None, :, None]
    is_new = (top_i >= past)[..., None, None]
    past_i = jnp.minimum(top_i, past - 1)
    phys = take(page_table, past_i // PAGE_SIZE)
    off = past_i % PAGE_SIZE
    new_i = jnp.clip(top_i - past, 0, T - 1)
    kg = jnp.where(is_new, take(k, new_i), cache_k[phys, off].astype(k.dtype))
    vg = jnp.where(is_new, take(v, new_i), cache_v[phys, off].astype(v.dtype))
    return sparse_attend(q, kg, vg, valid)


def trunk_layer(x, pos, conv_state, h0, S0, attn_fn, lw):
    (norm_mix, w_in, conv_w, conv_b, rg_wa, rg_ba, rg_wx, rg_bx, rg_lambda, q_norm, k_norm,
     gla_wa2, gla_ba, gla_norm, w_out, norm_mlp, w_up, w_down) = lw
    Bn, T, _ = x.shape
    xn = rmsnorm(x, norm_mix)
    (a_gate, a_x, b_q, b_k, b_v, b_qi, b_ki, b_wi, c_q, c_k, c_v, c_g, c_a) = split_cols(xn @ w_in)
    y_a, conv_new, h_new = rglru_mixer(a_gate, a_x, conv_state, h0, conv_w, conv_b,
                                       rg_wa, rg_ba, rg_wx, rg_bx, rg_lambda)
    q = partial_rope(rmsnorm(b_q.reshape(Bn, T, B_HEADS, HEAD_DIM), q_norm), pos)
    k = partial_rope(rmsnorm(b_k.reshape(Bn, T, B_KV_HEADS, HEAD_DIM), k_norm), pos)
    v = b_v.reshape(Bn, T, B_KV_HEADS, HEAD_DIM)
    qi = partial_rope(b_qi.reshape(Bn, T, IDX_HEADS, IDX_DIM), pos)
    ki = partial_rope(b_ki[:, :, None, :], pos)[:, :, 0, :]
    wi = b_wi * (IDX_HEADS ** -0.5)
    y_b = attn_fn(q, k, v, qi, ki, wi)
    y_c, S_new = gla_mixer(c_q, c_k, c_v, c_g, c_a, S0, gla_wa2, gla_ba, gla_norm)
    x = x + jnp.concatenate([y_a, y_b, y_c], axis=-1) @ w_out
    hid = rmsnorm(x, norm_mlp) @ w_up
    x = x + jnp.square(jax.nn.relu(hid)) @ w_down
    return x, (k, v, ki, conv_new, h_new, S_new)


def setup_inputs(seed: int = 0) -> dict:
    key = jax.random.key(seed)
    ks = jax.random.split(key, 32)
    f32 = jnp.float32
    nrm = jax.random.normal
    n_pages = PAST_LEN // PAGE_SIZE
    n_used = DEC_BATCH * n_pages
    n_pool = n_used + max(1, n_used // 4)
    page_table = jax.random.permutation(ks[2], n_pool)[:n_used].astype(jnp.int32).reshape(DEC_BATCH, n_pages)
    u = jax.random.uniform(ks[16], (DEPTH, A_WIDTH), f32, 0.9, 0.999)
    a_base = u ** (1.0 / RG_C)
    rg_lambda = jnp.log(a_base) - jnp.log1p(-a_base)
    return {
        "x_prompt": nrm(ks[0], (BATCH, SEQ, D_MODEL), f32),
        "x_sample": nrm(ks[1], (DEC_BATCH, DEC_SEQ, D_MODEL), f32),
        "cache_k": nrm(ks[3], (DEPTH, n_pool, PAGE_SIZE, B_KV_HEADS, HEAD_DIM), f32),
        "cache_v": nrm(ks[4], (DEPTH, n_pool, PAGE_SIZE, B_KV_HEADS, HEAD_DIM), f32),
        "cache_ki": nrm(ks[5], (DEPTH, n_pool, PAGE_SIZE, IDX_DIM), f32),
        "state_conv": nrm(ks[6], (DEPTH, DEC_BATCH, CONV_W - 1, A_WIDTH), f32),
        "state_rglru": 0.5 * nrm(ks[7], (DEPTH, DEC_BATCH, A_WIDTH), f32),
        "state_gla": 0.5 * nrm(ks[8], (DEPTH, DEC_BATCH, C_HEADS, C_DK, C_DV), f32),
        "page_table": page_table,
        "norm_mix": 1.0 + 0.02 * nrm(ks[9], (DEPTH, D_MODEL), f32),
        "w_in": nrm(ks[10], (DEPTH, D_MODEL, N_IN), f32) * D_MODEL ** -0.5,
        "conv_w": nrm(ks[11], (DEPTH, CONV_W, A_WIDTH), f32) * CONV_W ** -0.5,
        "conv_b": 0.02 * nrm(ks[12], (DEPTH, A_WIDTH), f32),
        "rg_wa": nrm(ks[13], (DEPTH, A_BLOCKS, A_BLOCK, A_BLOCK), f32) * A_BLOCK ** -0.5,
        "rg_ba": 0.02 * nrm(ks[14], (DEPTH, A_WIDTH), f32),
        "rg_wx": nrm(ks[15], (DEPTH, A_BLOCKS, A_BLOCK, A_BLOCK), f32) * A_BLOCK ** -0.5,
        "rg_bx": 0.02 * nrm(ks[17], (DEPTH, A_WIDTH), f32),
        "rg_lambda": rg_lambda,
        "q_norm": 1.0 + 0.02 * nrm(ks[18], (DEPTH, HEAD_DIM), f32),
        "k_norm": 1.0 + 0.02 * nrm(ks[19], (DEPTH, HEAD_DIM), f32),
        "gla_wa2": nrm(ks[20], (DEPTH, GATE_RANK, C_HEADS * C_DK), f32) * GATE_RANK ** -0.5,
        "gla_ba": 0.02 * nrm(ks[21], (DEPTH, C_HEADS * C_DK), f32),
        "gla_norm": 1.0 + 0.02 * nrm(ks[22], (DEPTH, C_DV), f32),
        "w_out": nrm(ks[23], (DEPTH, MIX_WIDTH, D_MODEL), f32) * MIX_WIDTH ** -0.5,
        "norm_mlp": 1.0 + 0.02 * nrm(ks[24], (DEPTH, D_MODEL), f32),
        "w_up": nrm(ks[25], (DEPTH, D_MODEL, D_FF), f32) * D_MODEL ** -0.5,
        "w_down": nrm(ks[26], (DEPTH, D_FF, D_MODEL), f32) * D_FF ** -0.5,
    }


def reference(x_prompt, x_sample, cache_k, cache_v, cache_ki, state_conv, state_rglru, state_gla, page_table,
              norm_mix, w_in, conv_w, conv_b, rg_wa, rg_ba, rg_wx, rg_bx, rg_lambda, q_norm, k_norm,
              gla_wa2, gla_ba, gla_norm, w_out, norm_mlp, w_up, w_down):
    layer_w = (norm_mix, w_in, conv_w, conv_b, rg_wa, rg_ba, rg_wx, rg_bx, rg_lambda, q_norm, k_norm,
               gla_wa2, gla_ba, gla_norm, w_out, norm_mlp, w_up, w_down)
    Bp, Tp = x_prompt.shape[0], x_prompt.shape[1]
    Ts = x_sample.shape[1]
    pos_p = jnp.arange(Tp)
    pos_s = PAST_LEN + jnp.arange(Ts)
    dt = x_prompt.dtype
    conv0 = jnp.zeros((Bp, CONV_W - 1, A_WIDTH), dt)
    h00 = jnp.zeros((Bp, A_WIDTH), dt)
    S00 = jnp.zeros((Bp, C_HEADS, C_DK, C_DV), dt)
    yp, ys = x_prompt, x_sample
    sp, ss = [], []
    for l in range(DEPTH):
        lw = tuple(w[l] for w in layer_w)
        yp, st_p = trunk_layer(yp, pos_p, conv0, h00, S00, dsa_prompt, lw)
        ck, cv, cki = cache_k[l], cache_v[l], cache_ki[l]
        attn_s = lambda q, k, v, qi, ki, wi, ck=ck, cv=cv, cki=cki: dsa_sample(q, k, v, qi, ki, wi, ck, cv, cki, page_table)
        ys, st_s = trunk_layer(ys, pos_s, state_conv[l], state_rglru[l], state_gla[l], attn_s, lw)
        sp.append(st_p)
        ss.append(st_s)
    k_prompt = jnp.stack([s[0] for s in sp])
    v_prompt = jnp.stack([s[1] for s in sp])
    ki_prompt = jnp.stack([s[2] for s in sp])
    conv_prompt = jnp.stack([s[3] for s in sp])
    h_prompt = jnp.stack([s[4] for s in sp])
    gla_prompt = jnp.stack([s[5] for s in sp])
    k_sample = jnp.stack([s[0] for s in ss])
    v_sample = jnp.stack([s[1] for s in ss])
    ki_sample = jnp.stack([s[2] for s in ss])
    conv_sample = jnp.stack([s[3] for s in ss])
    h_sample = jnp.stack([s[4] for s in ss])
    gla_sample = jnp.stack([s[5] for s in ss])
    return (yp, ys, k_prompt, v_prompt, ki_prompt, conv_prompt, h_prompt, gla_prompt,
            k_sample, v_sample, ki_sample, conv_sample, h_sample, gla_sample)
```

```python
import functools
import math

import jax
import jax.numpy as jnp
from jax import lax
from jax.experimental import pallas as pl
from jax.experimental.pallas import tpu as pltpu

F32 = jnp.float32
BF16 = jnp.bfloat16
I32 = jnp.int32

D_MODEL = 1024
HEAD_DIM = 64
EPS = 1e-6
A_WIDTH = 256
A_BLOCKS = 4
CONV_W = 4
RG_C = 8.0
B_HEADS = 8
B_KV_HEADS = 4
B_WIDTH = 512
KV_WIDTH = B_KV_HEADS * HEAD_DIM
ROPE_DIM = 16
ROPE_THETA = 500000.0
IDX_HEADS = 4
IDX_DIM = 64
TOPK_MAX = 256
C_WIDTH = 256
C_HEADS = 4
C_DV = 64
C_DK = 32
C_KW = C_HEADS * C_DK
GATE_RANK = 16
GATE_TAU = 16.0
CHUNK = 64
D_FF = 4 * D_MODEL
PAGE_SIZE = 128

LANES = 128
SUBLANES = 8
KEY_BLOCK = 128
Q_BLOCK = 128
MLP_ROWS = 256
VMEM_LIMIT = 56 * 1024 * 1024

COL_A = 0
COL_Q = 512
COL_K = 1024
COL_V = 1280
COL_QI = 1536
COL_C = 1792
COL_MISC = 2560
N_PERM = 2688
MISC_CA = 64
MISC_WI = 80

INT_MIN = -(2 ** 31)
NEG_BIG = -1e30
NT_DIMS = (((1,), (1,)), ((), ()))


def _dot(a, b):
    return jnp.dot(a, b, preferred_element_type=F32)


def _dot_nt(a, b):
    return lax.dot_general(a, b, NT_DIMS, preferred_element_type=F32)


def _group_mean_sq(t, bd):
    x2 = t * t
    hi = x2.astype(BF16)
    lo = (x2 - hi.astype(F32)).astype(BF16)
    return _dot(hi, bd) + _dot(lo, bd)


def _rope(t, cos, sin_a, sin_b):
    n = t.shape[1]
    reps = n // LANES
    c = jnp.tile(cos, (1, reps))
    sa = jnp.tile(sin_a, (1, reps))
    sb = jnp.tile(sin_b, (1, reps))
    return t * c + pltpu.roll(t, n - ROPE_DIM // 2, 1) * sa + pltpu.roll(t, ROPE_DIM // 2, 1) * sb


def _float_key(s):
    bits = lax.bitcast_convert_type(s, I32)
    return bits ^ ((bits >> 31) & 0x7FFFFFFF)


def _inproj_kernel(x_ref, g_ref, w_ref, cos_ref, sa_ref, sb_ref, qn_ref, kn_ref, bdq_ref, bdk_ref,
                   oa_ref, oq_ref, ok_ref, okb_ref, ov_ref, ovb_ref, oqi_ref, omisc_ref, oki2_ref, oc_ref):
    x = x_ref[...]
    ms = jnp.mean(x * x, axis=-1, keepdims=True)
    xn = (x * lax.rsqrt(ms + EPS) * g_ref[...]).astype(BF16)

    def proj(lo, hi):
        return _dot(xn, w_ref[:, lo:hi])

    cos = cos_ref[...]
    sa = sa_ref[...]
    sb = sb_ref[...]

    oa_ref[...] = proj(COL_A, COL_Q)
    oc_ref[...] = proj(COL_C, COL_MISC)

    q = proj(COL_Q, COL_K)
    q = q * lax.rsqrt(_group_mean_sq(q, bdq_ref[...]) + EPS) * qn_ref[...]
    q = _rope(q, cos, sa, sb)
    oq_ref[...] = (q * (HEAD_DIM ** -0.5)).astype(BF16)

    k = proj(COL_K, COL_V)
    k = k * lax.rsqrt(_group_mean_sq(k, bdk_ref[...]) + EPS) * kn_ref[...]
    k = _rope(k, cos, sa, sb)
    ok_ref[...] = k
    okb_ref[...] = k.astype(BF16)

    v = proj(COL_V, COL_QI)
    ov_ref[...] = v
    ovb_ref[...] = v.astype(BF16)

    qi = _rope(proj(COL_QI, COL_C), cos, sa, sb)
    oqi_ref[...] = (qi * (IDX_DIM ** -0.5)).astype(BF16)

    misc = proj(COL_MISC, N_PERM)
    lane = lax.broadcasted_iota(I32, misc.shape, 1)
    is_ki = lane < IDX_DIM
    misc = _rope(misc, jnp.where(is_ki, cos, 1.0), jnp.where(is_ki, sa, 0.0), jnp.where(is_ki, sb, 0.0))
    is_wi = (lane >= MISC_WI) & (lane < MISC_WI + IDX_HEADS)
    misc = jnp.where(is_wi, misc * (IDX_HEADS ** -0.5), misc)
    omisc_ref[...] = misc
    oki2_ref[...] = jnp.where(is_ki, misc, pltpu.roll(misc, IDX_DIM, 1)).astype(BF16)


def _inproj(x, g, w_perm, tabs, qn, kn, bdq, bdk, tm):
    m = x.shape[0]
    cos, sa, sb = tabs
    nper = cos.shape[0] // tm
    row = lambda n: pl.BlockSpec((tm, n), lambda i: (i, 0))
    full = lambda a: pl.BlockSpec(a.shape, lambda i: (0,) * a.ndim)
    tab = pl.BlockSpec((tm, LANES), lambda i: (i % nper, 0))
    widths = [(512, F32), (512, BF16), (KV_WIDTH, F32), (KV_WIDTH, BF16), (KV_WIDTH, F32), (KV_WIDTH, BF16),
              (256, BF16), (LANES, F32), (LANES, BF16), (768, F32)]
    return pl.pallas_call(
        _inproj_kernel,
        grid=(m // tm,),
        in_specs=[row(D_MODEL), full(g), full(w_perm), tab, tab, tab, full(qn), full(kn), full(bdq), full(bdk)],
        out_specs=[row(n) for n, _ in widths],
        out_shape=[jax.ShapeDtypeStruct((m, n), dt) for n, dt in widths],
        compiler_params=pltpu.CompilerParams(dimension_semantics=("arbitrary",), vmem_limit_bytes=VMEM_LIMIT),
        name="inproj",
    )(x, g, w_perm, cos, sa, sb, qn, kn, bdq, bdk)


def _rglru_kernel(a_ref, cs_ref, h0_ref, cw_ref, cb_ref, wa_ref, ba_ref, wx_ref, bx_ref, lam_ref,
                  y_ref, cnew_ref, hlast_ref, ext_ref, hcar_ref, *, tc):
    j = pl.program_id(1)

    @pl.when(j == 0)
    def _():
        ext_ref[0:SUBLANES, :] = cs_ref[0]
        hcar_ref[...] = h0_ref[0]

    ag = a_ref[:, 0:A_WIDTH]
    ext_ref[SUBLANES:SUBLANES + tc, :] = a_ref[:, A_WIDTH:2 * A_WIDTH]
    u = cb_ref[...]
    for jj in range(CONV_W):
        lo = SUBLANES - (CONV_W - 1) + jj
        u = u + ext_ref[lo:lo + tc, :] * cw_ref[jj:jj + 1, :]
    tail = ext_ref[tc:tc + SUBLANES, :]
    ext_ref[0:SUBLANES, :] = tail
    cnew_ref[0] = tail

    ub = u.astype(BF16)
    r = jax.nn.sigmoid(_dot(ub, wa_ref[...]) + ba_ref[...])
    gi = jax.nn.sigmoid(_dot(ub, wx_ref[...]) + bx_ref[...])
    nl = -lam_ref[...]
    softplus = jnp.maximum(nl, 0.0) + jnp.log1p(jnp.exp(-jnp.abs(nl)))
    log_a = -RG_C * r * softplus
    a = jnp.exp(log_a)
    b = jnp.sqrt(-jnp.tanh(log_a) * (a * a + 1.0)) * gi * u

    row = lax.broadcasted_iota(I32, a.shape, 0)
    s = 1
    while s < tc:
        keep = row >= s
        a_s = jnp.where(keep, pltpu.roll(a, s, 0), 1.0)
        b_s = jnp.where(keep, pltpu.roll(b, s, 0), 0.0)
        b = a * b_s + b
        a = a * a_s
        s *= 2
    h = a * hcar_ref[SUBLANES - 1:SUBLANES, :] + b
    htail = h[tc - SUBLANES:tc, :]
    hcar_ref[...] = htail
    hlast_ref[0] = htail
    y_ref[...] = (jax.nn.gelu(ag) * h).astype(BF16)


def _rglru(oa, cs8, h08, cw8, cb, wa, ba, wx, bx, lam, nb, t, tc):
    m = oa.shape[0]
    nj = t // tc
    full = lambda a: pl.BlockSpec(a.shape, lambda b, j: (0,) * a.ndim)
    per_b = pl.BlockSpec((1, SUBLANES, A_WIDTH), lambda b, j: (b, 0, 0))
    return pl.pallas_call(
        functools.partial(_rglru_kernel, tc=tc),
        grid=(nb, nj),
        in_specs=[pl.BlockSpec((tc, 2 * A_WIDTH), lambda b, j: (b * nj + j, 0)), per_b, per_b,
                  full(cw8), full(cb), full(wa), full(ba), full(wx), full(bx), full(lam)],
        out_specs=[pl.BlockSpec((tc, A_WIDTH), lambda b, j: (b * nj + j, 0)), per_b, per_b],
        out_shape=[jax.ShapeDtypeStruct((m, A_WIDTH), BF16),
                   jax.ShapeDtypeStruct((nb, SUBLANES, A_WIDTH), F32),
                   jax.ShapeDtypeStruct((nb, SUBLANES, A_WIDTH), F32)],
        scratch_shapes=[pltpu.VMEM((SUBLANES + tc, A_WIDTH), F32), pltpu.VMEM((SUBLANES, A_WIDTH), F32)],
        compiler_params=pltpu.CompilerParams(dimension_semantics=("arbitrary", "arbitrary"),
                                             vmem_limit_bytes=VMEM_LIMIT),
        name="rglru",
    )(oa, cs8, h08, cw8, cb, wa, ba, wx, bx, lam)


def _gla_kernel(c_ref, misc_ref, s0_ref, wa2_ref, gba_ref, gn_ref, bdv_ref, y_ref, sout_ref, st_ref, *, ch, nch):
    j = pl.program_id(1)

    @pl.when(j == 0)
    def _():
        st_ref[...] = s0_ref[0]

    row = lax.broadcasted_iota(I32, (ch, ch), 0)
    col = lax.broadcasted_iota(I32, (ch, ch), 1)
    causal = col <= row
    tri = jnp.where(causal, 1.0, 0.0).astype(F32)
    klane = lax.broadcasted_iota(I32, (ch, C_KW), 1)
    vlane = lax.broadcasted_iota(I32, (ch, C_WIDTH), 1)

    for c in range(nch):
        r0 = c * ch
        cq = c_ref[r0:r0 + ch, 0:C_KW]
        ck = c_ref[r0:r0 + ch, C_KW:2 * C_KW]
        cv = c_ref[r0:r0 + ch, 2 * C_KW:2 * C_KW + C_WIDTH]
        cg = c_ref[r0:r0 + ch, 2 * C_KW + C_WIDTH:2 * C_KW + 2 * C_WIDTH]
        z = _dot(misc_ref[r0:r0 + ch, :].astype(BF16), wa2_ref[...]) + gba_ref[...]
        lg = (jnp.minimum(z, 0.0) - jnp.log1p(jnp.exp(-jnp.abs(z)))) * (1.0 / GATE_TAU)
        b = jnp.dot(tri, lg, preferred_element_type=F32, precision=lax.Precision.HIGHEST)
        b_last = b[ch - 1:ch, :]
        qe = (cq * (C_DK ** -0.5) * jnp.exp(b)).astype(BF16)
        ke = (ck * jnp.exp(-b)).astype(BF16)
        kd = (ck * jnp.exp(b_last - b)).astype(BF16)
        vb = cv.astype(BF16)
        st = st_ref[...]
        o = _dot_nt(qe, st.astype(BF16))
        for h in range(C_HEADS):
            qh = jnp.where((klane >= h * C_DK) & (klane < (h + 1) * C_DK), qe, jnp.zeros_like(qe))
            att = jnp.where(causal, _dot_nt(qh, ke), 0.0).astype(BF16)
            oh = _dot(att, vb)
            o = o + jnp.where((vlane >= h * C_DV) & (vlane < (h + 1) * C_DV), oh, 0.0)
        ds = lax.dot_general(vb, kd, (((0,), (0,)), ((), ())), preferred_element_type=F32)
        srow = lax.broadcasted_iota(I32, ds.shape, 0) // C_DV
        scol = lax.broadcasted_iota(I32, ds.shape, 1) // C_DK
        st_ref[...] = st * jnp.exp(b_last) + jnp.where(srow == scol, ds, 0.0)
        on = o * lax.rsqrt(_group_mean_sq(o, bdv_ref[...]) + EPS) * gn_ref[...]
        y_ref[r0:r0 + ch, :] = (on * (cg * jax.nn.sigmoid(cg))).astype(BF16)

    sout_ref[0] = st_ref[...]


def _gla(oc, misc, st0, wa2p, gba, gn, bdv, nb, t, ch, nch):
    m = oc.shape[0]
    tg = ch * nch
    nj = t // tg
    full = lambda a: pl.BlockSpec(a.shape, lambda b, j: (0,) * a.ndim)
    per_b = pl.BlockSpec((1, C_WIDTH, C_KW), lambda b, j: (b, 0, 0))
    return pl.pallas_call(
        functools.partial(_gla_kernel, ch=ch, nch=nch),
        grid=(nb, nj),
        in_specs=[pl.BlockSpec((tg, 768), lambda b, j: (b * nj + j, 0)),
                  pl.BlockSpec((tg, LANES), lambda b, j: (b * nj + j, 0)),
                  per_b, full(wa2p), full(gba), full(gn), full(bdv)],
        out_specs=[pl.BlockSpec((tg, C_WIDTH), lambda b, j: (b * nj + j, 0)), per_b],
        out_shape=[jax.ShapeDtypeStruct((m, C_WIDTH), BF16), jax.ShapeDtypeStruct((nb, C_WIDTH, C_KW), F32)],
        scratch_shapes=[pltpu.VMEM((C_WIDTH, C_KW), F32)],
        compiler_params=pltpu.CompilerParams(dimension_semantics=("arbitrary", "arbitrary"),
                                             vmem_limit_bytes=VMEM_LIMIT),
        name="gla",
    )(oc, misc, st0, wa2p, gba, gn, bdv)


def _radix_select(count_ge, shape, k):
    def bit_body(it, t_u):
        cand_u = t_u | jnp.left_shift(jnp.int32(1), 31 - it)
        tot = count_ge(cand_u ^ INT_MIN)
        return jnp.where(tot >= k, cand_u, t_u)
    t_u = lax.fori_loop(0, 32, bit_body, jnp.zeros(shape, I32))
    return t_u ^ INT_MIN


def _select_mask(key, t_s, need, run_tie, upper):
    gt = key > t_s
    eq = key == t_s
    eqb = jnp.where(eq, 1.0, 0.0).astype(BF16)
    before = _dot(eqb, upper)
    sel = gt | (eq & (run_tie + before < need))
    return sel, run_tie + before[:, LANES - 1:LANES] + jnp.where(eq[:, LANES - 1:LANES], 1.0, 0.0)


def _strict_upper():
    r = lax.broadcasted_iota(I32, (KEY_BLOCK, KEY_BLOCK), 0)
    c = lax.broadcasted_iota(I32, (KEY_BLOCK, KEY_BLOCK), 1)
    return jnp.where(r < c, 1.0, 0.0).astype(BF16)


def _dsa_prompt_kernel(qi_ref, wi_ref, q_ref, ki_ref, k_ref, v_ref, o_ref,
                       key_ref, m_ref, l_ref, acc_ref, *, topk):
    i = pl.program_id(1)
    nkb = i + 1
    qb = Q_BLOCK
    lane = lax.broadcasted_iota(I32, (qb, LANES), 1)
    qrow = lax.broadcasted_iota(I32, (qb, LANES), 0)
    low_half = lane < HEAD_DIM

    qi = qi_ref[0]
    wi = wi_ref[0]
    zero_b = jnp.zeros((qb, LANES), BF16)
    qi_rows = jnp.concatenate(
        [jnp.where(low_half if h % 2 == 0 else ~low_half, qi[:, LANES * (h // 2):LANES * (h // 2 + 1)], zero_b)
         for h in range(IDX_HEADS)], axis=0)
    wib = [jnp.broadcast_to(wi[:, h:h + 1], (qb, LANES)) for h in range(IDX_HEADS)]

    def score_body(kb, carry):
        off = pl.multiple_of(kb * KEY_BLOCK, KEY_BLOCK)
        d = _dot_nt(qi_rows, ki_ref[0, pl.ds(off, KEY_BLOCK), :])
        s = wib[0] * jnp.maximum(d[0:qb], 0.0)
        for h in range(1, IDX_HEADS):
            s = s + wib[h] * jnp.maximum(d[h * qb:(h + 1) * qb], 0.0)
        causal = off + lane <= i * qb + qrow
        key_ref[kb] = jnp.where(causal, _float_key(s), INT_MIN)
        return carry
    lax.fori_loop(0, nkb, score_body, 0)

    def count(pred):
        def body(kb, c):
            return c + jnp.where(pred(key_ref[kb]), 1.0, 0.0)
        c = lax.fori_loop(0, nkb, body, jnp.zeros((qb, LANES), F32))
        return jnp.sum(c, axis=1, keepdims=True)

    t_s = _radix_select(lambda cand: count(lambda key: key >= cand), (qb, LANES), float(topk))
    need = float(topk) - count(lambda key: key > t_s)

    q = q_ref[0]
    q_rows = []
    for g in range(B_KV_HEADS):
        blk = q[:, LANES * g:LANES * (g + 1)]
        want_low = g % 2 == 0
        for h2 in range(2):
            src = blk if (h2 == 0) == want_low else pltpu.roll(blk.astype(F32), HEAD_DIM, 1).astype(BF16)
            q_rows.append(jnp.where(low_half if want_low else ~low_half, src, zero_b))
    m_ref[...] = jnp.full(m_ref.shape, NEG_BIG, F32)
    l_ref[...] = jnp.zeros(l_ref.shape, F32)
    acc_ref[...] = jnp.zeros(acc_ref.shape, F32)
    upper = _strict_upper()

    def attn_body(kb, run_tie):
        off = pl.multiple_of(kb * KEY_BLOCK, KEY_BLOCK)
        key = key_ref[kb]
        sel, run_tie = _select_mask(key, t_s, need, run_tie, upper)
        sel = sel & (off + lane <= i * qb + qrow)
        sel2 = jnp.concatenate([sel, sel], axis=0)
        kblk = k_ref[0, pl.ds(off, KEY_BLOCK), :]
        vblk = v_ref[0, pl.ds(off, KEY_BLOCK), :]
        for g in range(B_KV_HEADS):
            pair = slice(LANES * (g // 2), LANES * (g // 2 + 1))
            lhs = jnp.concatenate([q_rows[2 * g], q_rows[2 * g + 1]], axis=0)
            logit = _dot_nt(lhs, kblk[:, pair])
            m_old = m_ref[g]
            m_new = jnp.maximum(m_old, jnp.max(jnp.where(sel2, logit, NEG_BIG), axis=1, keepdims=True))
            p = jnp.where(sel2, jnp.exp(logit - m_new), 0.0)
            alpha = jnp.exp(m_old - m_new)
            l_ref[g] = alpha * l_ref[g] + jnp.sum(p, axis=1, keepdims=True)
            acc_ref[g] = alpha * acc_ref[g] + _dot(p.astype(BF16), vblk[:, pair])
            m_ref[g] = m_new
        return run_tie
    lax.fori_loop(0, nkb, attn_body, jnp.zeros((qb, 1), F32))

    for g in range(B_KV_HEADS):
        og = acc_ref[g] / l_ref[g]
        want_low = g % 2 == 0
        halves = []
        for h2 in range(2):
            part = og[h2 * qb:(h2 + 1) * qb]
            halves.append(part if (h2 == 0) == want_low else pltpu.roll(part, HEAD_DIM, 1))
        o_ref[0, :, LANES * g:LANES * (g + 1)] = jnp.where(low_half, halves[0], halves[1]).astype(BF16)


def _dsa_prompt(qi, wi, q, ki2, kb16, vb16, nb, t):
    topk = min(TOPK_MAX, t // 4)
    nq = t // Q_BLOCK
    nkb = t // KEY_BLOCK
    qspec = lambda n: pl.BlockSpec((1, Q_BLOCK, n), lambda b, i: (b, i, 0))
    kspec = lambda n: pl.BlockSpec((1, t, n), lambda b, i: (b, 0, 0))
    return pl.pallas_call(
        functools.partial(_dsa_prompt_kernel, topk=topk),
        grid=(nb, nq),
        in_specs=[qspec(256), qspec(IDX_HEADS), qspec(B_WIDTH), kspec(LANES), kspec(KV_WIDTH), kspec(KV_WIDTH)],
        out_specs=qspec(B_WIDTH),
        out_shape=jax.ShapeDtypeStruct((nb, t, B_WIDTH), BF16),
        scratch_shapes=[pltpu.VMEM((nkb, Q_BLOCK, KEY_BLOCK), I32),
                        pltpu.VMEM((B_KV_HEADS, 2 * Q_BLOCK, LANES), F32),
                        pltpu.VMEM((B_KV_HEADS, 2 * Q_BLOCK, LANES), F32),
                        pltpu.VMEM((B_KV_HEADS, 2 * Q_BLOCK, LANES), F32)],
        compiler_params=pltpu.CompilerParams(dimension_semantics=("arbitrary", "arbitrary"),
                                             vmem_limit_bytes=VMEM_LIMIT),
        name="dsa_prompt",
    )(qi, wi, q, ki2, kb16, vb16)


def _page_score_kernel(pt_ref, qi_ref, wi_ref, page_ref, s_ref):
    qi = qi_ref[0]
    wi = wi_ref[0]
    kp = page_ref[0].astype(BF16)
    s = None
    for h in range(IDX_HEADS):
        d = _dot_nt(qi[:, IDX_DIM * h:IDX_DIM * (h + 1)], kp)
        term = wi[:, h:h + 1] * jnp.maximum(d, 0.0)
        s = term if s is None else s + term
    s_ref[0, 0] = s


def _page_scores(page_table, qi, wi, cache_ki):
    nb, npg = page_table.shape
    t = qi.shape[1]
    grid_spec = pltpu.PrefetchScalarGridSpec(
        num_scalar_prefetch=1,
        grid=(nb, npg),
        in_specs=[pl.BlockSpec((1, t, 256), lambda b, p, pt: (b, 0, 0)),
                  pl.BlockSpec((1, t, IDX_HEADS), lambda b, p, pt: (b, 0, 0)),
                  pl.BlockSpec((1, PAGE_SIZE, IDX_DIM), lambda b, p, pt: (pt[b, p], 0, 0))],
        out_specs=pl.BlockSpec((1, 1, t, PAGE_SIZE), lambda b, p, pt: (b, p, 0, 0)),
    )
    return pl.pallas_call(
        _page_score_kernel,
        grid_spec=grid_spec,
        out_shape=jax.ShapeDtypeStruct((nb, npg, t, PAGE_SIZE), F32),
        compiler_params=pltpu.CompilerParams(dimension_semantics=("arbitrary", "arbitrary"),
                                             vmem_limit_bytes=VMEM_LIMIT),
        name="page_scores",
    )(page_table, qi, wi, cache_ki)


def _dsa_sample_kernel(pt_ref, s_ref, qi_ref, wi_ref, kin_ref, q_ref, kn_ref, vn_ref, kpage_ref, vpage_ref,
                       o_ref, key_ref, thr_ref, need_ref, tie_ref, qx_ref, m_ref, l_ref, acc_ref, *, topk, npg, t):
    p = pl.program_id(1)
    lane = lax.broadcasted_iota(I32, (t, LANES), 1)
    trow = lax.broadcasted_iota(I32, (t, LANES), 0)
    new_ok = lane <= trow
    nrows = B_HEADS * t
    upper = _strict_upper()

    @pl.when(p == 0)
    def _():
        def key_body(pp, carry):
            key_ref[pp] = _float_key(s_ref[0, pp])
            return carry
        lax.fori_loop(0, npg, key_body, 0)
        qi = qi_ref[0]
        wi = wi_ref[0]
        kin = kin_ref[0]
        s = None
        for h in range(IDX_HEADS):
            d = _dot_nt(qi[:, IDX_DIM * h:IDX_DIM * (h + 1)], kin)
            term = wi[:, h:h + 1] * jnp.maximum(d, 0.0)
            s = term if s is None else s + term
        key_ref[npg] = jnp.where(new_ok, _float_key(s), INT_MIN)

        def count(pred):
            def body(pp, c):
                return c + jnp.where(pred(key_ref[pp]), 1.0, 0.0)
            c = lax.fori_loop(0, npg + 1, body, jnp.zeros((t, LANES), F32))
            return jnp.sum(c, axis=1, keepdims=True)

        t_s = _radix_select(lambda cand: count(lambda key: key >= cand), (t, LANES), float(topk))
        thr_ref[...] = t_s
        need_ref[...] = jnp.broadcast_to(float(topk) - count(lambda key: key > t_s), (t, LANES))
        tie_ref[...] = jnp.zeros(tie_ref.shape, F32)

        q = q_ref[0]
        low_half = lane < HEAD_DIM
        zero_b = jnp.zeros((t, LANES), BF16)
        for h in range(B_HEADS):
            g = h // 2
            blk = q[:, LANES * g:LANES * (g + 1)]
            want_low = g % 2 == 0
            src = blk if (h % 2 == 0) == want_low else pltpu.roll(blk.astype(F32), HEAD_DIM, 1).astype(BF16)
            piece = jnp.where(low_half if want_low else ~low_half, src, zero_b)
            for pb in range(KV_WIDTH // LANES):
                qx_ref[h * t:(h + 1) * t, LANES * pb:LANES * (pb + 1)] = piece if pb == g // 2 else zero_b
        m_ref[...] = jnp.full(m_ref.shape, NEG_BIG, F32)
        l_ref[...] = jnp.zeros(l_ref.shape, F32)
        acc_ref[...] = jnp.zeros(acc_ref.shape, F32)

    def attend(key, kblk, vblk, visible):
        sel, run_tie = _select_mask(key, thr_ref[...], need_ref[...], tie_ref[:, 0:1], upper)
        tie_ref[...] = jnp.broadcast_to(run_tie, tie_ref.shape)
        if visible is not None:
            sel = sel & visible
        selr = jnp.tile(jnp.where(sel, 1.0, 0.0), (B_HEADS, 1)) > 0.5
        logit = _dot_nt(qx_ref[...], kblk)
        m_old = m_ref[...]
        m_new = jnp.maximum(m_old, jnp.max(jnp.where(selr, logit, NEG_BIG), axis=1, keepdims=True))
        pr = jnp.where(selr, jnp.exp(logit - m_new), 0.0)
        alpha = jnp.exp(m_old - m_new)
        l_ref[...] = alpha * l_ref[...] + jnp.sum(pr, axis=1, keepdims=True)
        acc_ref[...] = jnp.tile(alpha, (1, KV_WIDTH // LANES)) * acc_ref[...] + _dot(pr.astype(BF16), vblk)
        m_ref[...] = m_new

    attend(key_ref[p], kpage_ref[0].astype(BF16), vpage_ref[0].astype(BF16), None)

    @pl.when(p == npg - 1)
    def _():
        attend(key_ref[npg], kn_ref[0], vn_ref[0], new_ok)
        o_ref[0] = acc_ref[...] / jnp.tile(l_ref[...], (1, KV_WIDTH // LANES))


def _dsa_sample(page_table, scores, qi, wi, kin_pad, q, kn_pad, vn_pad, cache_k, cache_v):
    nb, npg = page_table.shape
    t = qi.shape[1]
    topk = min(TOPK_MAX, (npg * PAGE_SIZE + t) // 4)
    per_b = lambda shape: pl.BlockSpec((1,) + shape, lambda b, p, pt: (b,) + (0,) * len(shape))
    page = pl.BlockSpec((1, PAGE_SIZE, KV_WIDTH), lambda b, p, pt: (pt[b, p], 0, 0))
    nrows = B_HEADS * t
    grid_spec = pltpu.PrefetchScalarGridSpec(
        num_scalar_prefetch=1,
        grid=(nb, npg),
        in_specs=[per_b((npg, t, PAGE_SIZE)), per_b((t, 256)), per_b((t, IDX_HEADS)), per_b((KEY_BLOCK, IDX_DIM)),
                  per_b((t, B_WIDTH)), per_b((KEY_BLOCK, KV_WIDTH)), per_b((KEY_BLOCK, KV_WIDTH)), page, page],
        out_specs=per_b((nrows, KV_WIDTH)),
        scratch_shapes=[pltpu.VMEM((npg + 1, t, LANES), I32), pltpu.VMEM((t, LANES), I32),
                        pltpu.VMEM((t, LANES), F32), pltpu.VMEM((t, LANES), F32),
                        pltpu.VMEM((nrows, KV_WIDTH), BF16), pltpu.VMEM((nrows, LANES), F32),
                        pltpu.VMEM((nrows, LANES), F32), pltpu.VMEM((nrows, KV_WIDTH), F32)],
    )
    return pl.pallas_call(
        functools.partial(_dsa_sample_kernel, topk=topk, npg=npg, t=t),
        grid_spec=grid_spec,
        out_shape=jax.ShapeDtypeStruct((nb, nrows, KV_WIDTH), F32),
        compiler_params=pltpu.CompilerParams(dimension_semantics=("arbitrary", "arbitrary"),
                                             vmem_limit_bytes=VMEM_LIMIT),
        name="dsa_sample",
    )(page_table, scores, qi, wi, kin_pad, q, kn_pad, vn_pad, cache_k, cache_v)


def _outmlp_kernel(x_ref, ya_ref, yb_ref, yc_ref, wo_ref, g_ref, wu_ref, wd_ref, o_ref, *, ff_chunk):
    mix = (_dot(ya_ref[...], wo_ref[0:A_WIDTH, :]) + _dot(yb_ref[...], wo_ref[A_WIDTH:A_WIDTH + B_WIDTH, :])
           + _dot(yc_ref[...], wo_ref[A_WIDTH + B_WIDTH:, :]))
    x1 = x_ref[...] + mix
    ms = jnp.mean(x1 * x1, axis=-1, keepdims=True)
    xn = (x1 * lax.rsqrt(ms + EPS) * g_ref[...]).astype(BF16)
    mlp = None
    for c in range(D_FF // ff_chunk):
        hid = jnp.maximum(_dot(xn, wu_ref[:, c * ff_chunk:(c + 1) * ff_chunk]), 0.0)
        down = _dot((hid * hid).astype(BF16), wd_ref[c * ff_chunk:(c + 1) * ff_chunk, :])
        mlp = down if mlp is None else mlp + down
    o_ref[...] = x1 + mlp


def _outmlp(x, ya, yb, yc, wo, g, wu, wd, tm):
    m = x.shape[0]
    row = lambda n: pl.BlockSpec((tm, n), lambda i: (i, 0))
    full = lambda a: pl.BlockSpec(a.shape, lambda i: (0,) * a.ndim)
    return pl.pallas_call(
        functools.partial(_outmlp_kernel, ff_chunk=1024),
        grid=(m // tm,),
        in_specs=[row(D_MODEL), row(A_WIDTH), row(B_WIDTH), row(C_WIDTH), full(wo), full(g), full(wu), full(wd)],
        out_specs=row(D_MODEL),
        out_shape=jax.ShapeDtypeStruct((m, D_MODEL), F32),
        compiler_params=pltpu.CompilerParams(dimension_semantics=("arbitrary",), vmem_limit_bytes=VMEM_LIMIT),
        name="outmlp",
    )(x, ya, yb, yc, wo, g, wu, wd)


def _rope_tables(pos, rows):
    half = ROPE_DIM // 2
    inv_freq = ROPE_THETA ** (-(jnp.arange(half, dtype=F32) * 2.0 / ROPE_DIM))
    ang = pos.astype(F32)[:, None] * inv_freq[None, :]
    cos, sin = jnp.cos(ang), jnp.sin(ang)
    n = pos.shape[0]
    ones = jnp.ones((n, HEAD_DIM - ROPE_DIM), F32)
    zeros = jnp.zeros((n, HEAD_DIM - ROPE_DIM), F32)
    zh = jnp.zeros((n, half), F32)
    c = jnp.concatenate([cos, cos, ones], axis=1)
    sa = jnp.concatenate([-sin, zh, zeros], axis=1)
    sb = jnp.concatenate([zh, sin, zeros], axis=1)
    reps = max(rows // n, 1)
    return tuple(jnp.tile(jnp.concatenate([a, a], axis=1), (reps, 1)) for a in (c, sa, sb))


def _block_diag(blocks):
    n, r, c = blocks.shape
    eye = jnp.eye(n, dtype=blocks.dtype)
    return (blocks[:, :, None, :] * eye[:, None, :, None]).reshape(n * r, n * c)


def _group_mean_matrix(width):
    g = jnp.arange(width) // HEAD_DIM
    return jnp.where(g[:, None] == g[None, :], 1.0 / HEAD_DIM, 0.0).astype(BF16)


def _pad_rows(a, rows):
    return jnp.pad(a, ((0, 0), (0, rows - a.shape[1]), (0, 0)))


def _layer_weights(l, norm_mix, w_in, conv_w, conv_b, rg_wa, rg_ba, rg_wx, rg_bx, rg_lambda, q_norm, k_norm,
                   gla_wa2, gla_ba, gla_norm, w_out, norm_mlp, w_up, w_down):
    wi = w_in[l]
    w_perm = jnp.concatenate(
        [wi[:, 0:1792], wi[:, 1860:2628], wi[:, 1792:1856], wi[:, 2628:2644], wi[:, 1856:1860],
         jnp.zeros((D_MODEL, N_PERM - 2644), wi.dtype)], axis=1).astype(BF16)
    wa2p = jnp.zeros((LANES, C_KW), F32).at[MISC_CA:MISC_CA + GATE_RANK].set(gla_wa2[l]).astype(BF16)
    return dict(
        g_mix=norm_mix[l][None, :], w_perm=w_perm,
        qn=jnp.tile(q_norm[l], B_HEADS)[None, :], kn=jnp.tile(k_norm[l], B_KV_HEADS)[None, :],
        cw8=jnp.pad(conv_w[l], ((0, SUBLANES - CONV_W), (0, 0))), cb=conv_b[l][None, :],
        wa=_block_diag(rg_wa[l]).astype(BF16), ba=rg_ba[l][None, :],
        wx=_block_diag(rg_wx[l]).astype(BF16), bx=rg_bx[l][None, :], lam=rg_lambda[l][None, :],
        wa2p=wa2p, gba=gla_ba[l][None, :], gn=jnp.tile(gla_norm[l], C_HEADS)[None, :],
        wo=w_out[l].astype(BF16), g_mlp=norm_mlp[l][None, :], wu=w_up[l].astype(BF16), wd=w_down[l].astype(BF16),
    )


def _state_to_blockdiag(s):
    nb = s.shape[0]
    eye = jnp.eye(C_HEADS, dtype=s.dtype)
    st = jnp.swapaxes(s, 2, 3)[:, :, :, None, :] * eye[None, :, None, :, None]
    return st.reshape(nb, C_WIDTH, C_KW)


def _blockdiag_to_state(st):
    nb = st.shape[0]
    st = st.reshape(nb, C_HEADS, C_DV, C_HEADS, C_DK)
    s = jnp.stack([st[:, h, :, h, :] for h in range(C_HEADS)], axis=1)
    return jnp.swapaxes(s, 2, 3)


def _trunk_layer(x, nb, t, pos, conv_state, h0, s0, lw, consts, attn_fn, tm, tc, ch, nch):
    bdq, bdk, bdv = consts
    m = nb * t
    tabs = _rope_tables(pos, tm)
    oa, oq, ok, okb, ov, ovb, oqi, omisc, oki2, oc = _inproj(
        x, lw["g_mix"], lw["w_perm"], tabs, lw["qn"], lw["kn"], bdq, bdk, tm)

    cs8 = jnp.pad(conv_state, ((0, 0), (SUBLANES - (CONV_W - 1), 0), (0, 0)))
    h08 = jnp.broadcast_to(h0[:, None, :], (nb, SUBLANES, A_WIDTH))
    ya, cnew, hlast = _rglru(oa, cs8, h08, lw["cw8"], lw["cb"], lw["wa"], lw["ba"], lw["wx"], lw["bx"],
                             lw["lam"], nb, t, tc)

    yc, st = _gla(oc, omisc, _state_to_blockdiag(s0), lw["wa2p"], lw["gba"], lw["gn"], bdv, nb, t, ch, nch)

    wi = omisc[:, MISC_WI:MISC_WI + IDX_HEADS]
    yb = attn_fn(oq, okb, ovb, oqi, oki2, wi)

    y = _outmlp(x, ya, yb, yc, lw["wo"], lw["g_mlp"], lw["wu"], lw["wd"], min(tm, MLP_ROWS))
    states = (ok.reshape(nb, t, B_KV_HEADS, HEAD_DIM), ov.reshape(nb, t, B_KV_HEADS, HEAD_DIM),
              omisc[:, :IDX_DIM].reshape(nb, t, IDX_DIM), cnew[:, SUBLANES - (CONV_W - 1):],
              hlast[:, SUBLANES - 1], _blockdiag_to_state(st))
    return y, states


def kernel(x_prompt, x_sample, cache_k, cache_v, cache_ki, state_conv, state_rglru, state_gla, page_table,
           norm_mix, w_in, conv_w, conv_b, rg_wa, rg_ba, rg_wx, rg_bx, rg_lambda, q_norm, k_norm,
           gla_wa2, gla_ba, gla_norm, w_out, norm_mlp, w_up, w_down):
    bp, tp, _ = x_prompt.shape
    bs, ts, _ = x_sample.shape
    depth = w_in.shape[0]
    npg = page_table.shape[1]
    past = npg * PAGE_SIZE
    assert tp % 512 == 0 and ts == SUBLANES and tp >= CONV_W - 1
    n_pool = cache_k.shape[1]
    consts = (_group_mean_matrix(B_WIDTH), _group_mean_matrix(KV_WIDTH), _group_mean_matrix(C_WIDTH))
    pos_p = jnp.arange(tp)
    pos_s = past + jnp.arange(ts)
    dt = x_prompt.dtype
    conv0 = jnp.zeros((bp, CONV_W - 1, A_WIDTH), dt)
    h00 = jnp.zeros((bp, A_WIDTH), dt)
    s00 = jnp.zeros((bp, C_HEADS, C_DK, C_DV), dt)

    yp = x_prompt.reshape(bp * tp, D_MODEL)
    ys = x_sample.reshape(bs * ts, D_MODEL)
    sp, ss = [], []
    for l in range(depth):
        lw = _layer_weights(l, norm_mix, w_in, conv_w, conv_b, rg_wa, rg_ba, rg_wx, rg_bx, rg_lambda, q_norm,
                            k_norm, gla_wa2, gla_ba, gla_norm, w_out, norm_mlp, w_up, w_down)

        def attn_p(oq, okb, ovb, oqi, oki2, wi):
            r3 = lambda a: a.reshape(bp, tp, a.shape[-1])
            return _dsa_prompt(r3(oqi), r3(wi), r3(oq), r3(oki2), r3(okb), r3(ovb), bp, tp).reshape(bp * tp, B_WIDTH)

        ck = cache_k[l].reshape(n_pool, PAGE_SIZE, KV_WIDTH)
        cv = cache_v[l].reshape(n_pool, PAGE_SIZE, KV_WIDTH)
        cki = cache_ki[l]

        def attn_s(oq, okb, ovb, oqi, oki2, wi):
            r3 = lambda a: a.reshape(bs, ts, a.shape[-1])
            qi3, wi3 = r3(oqi), r3(wi)
            scores = _page_scores(page_table, qi3, wi3, cki)
            kin_pad = _pad_rows(r3(oki2)[:, :, :IDX_DIM], KEY_BLOCK)
            acc = _dsa_sample(page_table, scores, qi3, wi3, kin_pad, r3(oq), _pad_rows(r3(okb), KEY_BLOCK),
                              _pad_rows(r3(ovb), KEY_BLOCK), ck, cv)
            acc = acc.reshape(bs, B_HEADS, ts, B_KV_HEADS, HEAD_DIM)
            o = jnp.stack([acc[:, h, :, h // 2, :] for h in range(B_HEADS)], axis=2)
            return o.reshape(bs * ts, B_WIDTH).astype(BF16)

        yp, st_p = _trunk_layer(yp, bp, tp, pos_p, conv0, h00, s00, lw, consts, attn_p,
                                tm=512, tc=256, ch=math.gcd(tp, CHUNK), nch=512 // math.gcd(tp, CHUNK))
        ys, st_s = _trunk_layer(ys, bs, ts, pos_s, state_conv[l], state_rglru[l], state_gla[l], lw, consts, attn_s,
                                tm=bs * ts, tc=ts, ch=math.gcd(ts, CHUNK), nch=1)
        sp.append(st_p)
        ss.append(st_s)

    outs = [yp.reshape(bp, tp, D_MODEL), ys.reshape(bs, ts, D_MODEL)]
    for states in (sp, ss):
        for idx in range(6):
            outs.append(jnp.stack([s[idx] for s in states]))
    return tuple(outs)
```

```python
import functools
import math

import jax
import jax.numpy as jnp
from jax import lax
from jax.experimental import pallas as pl
from jax.experimental.pallas import tpu as pltpu

F32 = jnp.float32
BF16 = jnp.bfloat16
I32 = jnp.int32

D_MODEL = 1024
HEAD_DIM = 64
EPS = 1e-6
A_WIDTH = 256
A_BLOCKS = 4
CONV_W = 4
RG_C = 8.0
B_HEADS = 8
B_KV_HEADS = 4
B_WIDTH = 512
KV_WIDTH = B_KV_HEADS * HEAD_DIM
ROPE_DIM = 16
ROPE_THETA = 500000.0
IDX_HEADS = 4
IDX_DIM = 64
TOPK_MAX = 256
C_WIDTH = 256
C_HEADS = 4
C_DV = 64
C_DK = 32
C_KW = C_HEADS * C_DK
GATE_RANK = 16
GATE_TAU = 16.0
CHUNK = 64
D_FF = 4 * D_MODEL
PAGE_SIZE = 128

LANES = 128
SUBLANES = 8
KEY_BLOCK = 128
Q_BLOCK = 128
SCORE_PAGES = 16
ATTN_PAGES = 8
MLP_ROWS = 256
VMEM_LIMIT = 56 * 1024 * 1024

COL_A = 0
COL_Q = 512
COL_K = 1024
COL_V = 1280
COL_QI = 1536
COL_C = 1792
COL_MISC = 2560
N_PERM = 2688
MISC_CA = 64
MISC_WI = 80

INT_MIN = -(2 ** 31)
NEG_BIG = -1e30
LOG2E = math.log2(math.e)
NT_DIMS = (((1,), (1,)), ((), ()))


def _dot(a, b):
    return jnp.dot(a, b, preferred_element_type=F32)


def _dot_nt(a, b):
    return lax.dot_general(a, b, NT_DIMS, preferred_element_type=F32)


def _group_mean_sq(t, bd):
    x2 = t * t
    hi = x2.astype(BF16)
    lo = (x2 - hi.astype(F32)).astype(BF16)
    return _dot(hi, bd) + _dot(lo, bd)


def _rope(t, cos, sin_a, sin_b):
    n = t.shape[1]
    reps = n // LANES
    c = jnp.tile(cos, (1, reps))
    sa = jnp.tile(sin_a, (1, reps))
    sb = jnp.tile(sin_b, (1, reps))
    return t * c + pltpu.roll(t, n - ROPE_DIM // 2, 1) * sa + pltpu.roll(t, ROPE_DIM // 2, 1) * sb


def _float_key(s):
    bits = lax.bitcast_convert_type(s, I32)
    return bits ^ ((bits >> 31) & 0x7FFFFFFF)


def _inproj_kernel(x_ref, g_ref, w_ref, cos_ref, sa_ref, sb_ref, qn_ref, kn_ref, bdq_ref, bdk_ref,
                   oa_ref, oq_ref, ok_ref, okb_ref, ov_ref, ovb_ref, oqi_ref, omisc_ref, oki2_ref, oc_ref):
    x = x_ref[...]
    ms = jnp.mean(x * x, axis=-1, keepdims=True)
    xn = (x * lax.rsqrt(ms + EPS) * g_ref[...]).astype(BF16)

    def proj(lo, hi):
        return _dot(xn, w_ref[:, lo:hi])

    cos = cos_ref[...]
    sa = sa_ref[...]
    sb = sb_ref[...]

    oa_ref[...] = proj(COL_A, COL_Q)
    oc_ref[...] = proj(COL_C, COL_MISC)

    q = proj(COL_Q, COL_K)
    q = q * lax.rsqrt(_group_mean_sq(q, bdq_ref[...]) + EPS) * qn_ref[...]
    q = _rope(q, cos, sa, sb)
    oq_ref[...] = (q * (HEAD_DIM ** -0.5 * LOG2E)).astype(BF16)

    k = proj(COL_K, COL_V)
    k = k * lax.rsqrt(_group_mean_sq(k, bdk_ref[...]) + EPS) * kn_ref[...]
    k = _rope(k, cos, sa, sb)
    ok_ref[...] = k
    okb_ref[...] = k.astype(BF16)

    v = proj(COL_V, COL_QI)
    ov_ref[...] = v
    ovb_ref[...] = v.astype(BF16)

    qi = _rope(proj(COL_QI, COL_C), cos, sa, sb)
    oqi_ref[...] = (qi * (IDX_DIM ** -0.5)).astype(BF16)

    misc = proj(COL_MISC, N_PERM)
    lane = lax.broadcasted_iota(I32, misc.shape, 1)
    is_ki = lane < IDX_DIM
    misc = _rope(misc, jnp.where(is_ki, cos, 1.0), jnp.where(is_ki, sa, 0.0), jnp.where(is_ki, sb, 0.0))
    is_wi = (lane >= MISC_WI) & (lane < MISC_WI + IDX_HEADS)
    misc = jnp.where(is_wi, misc * (IDX_HEADS ** -0.5), misc)
    omisc_ref[...] = misc
    oki2_ref[...] = jnp.where(is_ki, misc, pltpu.roll(misc, IDX_DIM, 1)).astype(BF16)


def _inproj(x, g, w_perm, tabs, qn, kn, bdq, bdk, tm):
    m = x.shape[0]
    cos, sa, sb = tabs
    nper = cos.shape[0] // tm
    row = lambda n: pl.BlockSpec((tm, n), lambda i: (i, 0))
    full = lambda a: pl.BlockSpec(a.shape, lambda i: (0,) * a.ndim)
    tab = pl.BlockSpec((tm, LANES), lambda i: (i % nper, 0))
    widths = [(512, F32), (512, BF16), (KV_WIDTH, F32), (KV_WIDTH, BF16), (KV_WIDTH, F32), (KV_WIDTH, BF16),
              (256, BF16), (LANES, F32), (LANES, BF16), (768, F32)]
    return pl.pallas_call(
        _inproj_kernel,
        grid=(m // tm,),
        in_specs=[row(D_MODEL), full(g), full(w_perm), tab, tab, tab, full(qn), full(kn), full(bdq), full(bdk)],
        out_specs=[row(n) for n, _ in widths],
        out_shape=[jax.ShapeDtypeStruct((m, n), dt) for n, dt in widths],
        compiler_params=pltpu.CompilerParams(dimension_semantics=("arbitrary",), vmem_limit_bytes=VMEM_LIMIT),
        name="inproj",
    )(x, g, w_perm, cos, sa, sb, qn, kn, bdq, bdk)


def _rglru_kernel(a_ref, cs_ref, h0_ref, cw_ref, cb_ref, wa_ref, ba_ref, wx_ref, bx_ref, lam_ref,
                  y_ref, cnew_ref, hlast_ref, ext_ref, hcar_ref, *, tc):
    j = pl.program_id(1)

    @pl.when(j == 0)
    def _():
        ext_ref[0:SUBLANES, :] = cs_ref[0]
        hcar_ref[...] = h0_ref[0]

    ag = a_ref[:, 0:A_WIDTH]
    ext_ref[SUBLANES:SUBLANES + tc, :] = a_ref[:, A_WIDTH:2 * A_WIDTH]
    u = cb_ref[...]
    for jj in range(CONV_W):
        lo = SUBLANES - (CONV_W - 1) + jj
        u = u + ext_ref[lo:lo + tc, :] * cw_ref[jj:jj + 1, :]
    tail = ext_ref[tc:tc + SUBLANES, :]
    ext_ref[0:SUBLANES, :] = tail
    cnew_ref[0] = tail

    ub = u.astype(BF16)
    r = jax.nn.sigmoid(_dot(ub, wa_ref[...]) + ba_ref[...])
    gi = jax.nn.sigmoid(_dot(ub, wx_ref[...]) + bx_ref[...])
    nl = -lam_ref[...]
    softplus = jnp.maximum(nl, 0.0) + jnp.log1p(jnp.exp(-jnp.abs(nl)))
    log_a = -RG_C * r * softplus
    a = jnp.exp(log_a)
    b = jnp.sqrt(-jnp.tanh(log_a) * (a * a + 1.0)) * gi * u

    row = lax.broadcasted_iota(I32, a.shape, 0)
    s = 1
    while s < tc:
        keep = row >= s
        a_s = jnp.where(keep, pltpu.roll(a, s, 0), 1.0)
        b_s = jnp.where(keep, pltpu.roll(b, s, 0), 0.0)
        b = a * b_s + b
        a = a * a_s
        s *= 2
    h = a * hcar_ref[SUBLANES - 1:SUBLANES, :] + b
    htail = h[tc - SUBLANES:tc, :]
    hcar_ref[...] = htail
    hlast_ref[0] = htail
    y_ref[...] = (jax.nn.gelu(ag) * h).astype(BF16)


def _rglru(oa, cs8, h08, cw8, cb, wa, ba, wx, bx, lam, nb, t, tc):
    m = oa.shape[0]
    nj = t // tc
    full = lambda a: pl.BlockSpec(a.shape, lambda b, j: (0,) * a.ndim)
    per_b = pl.BlockSpec((1, SUBLANES, A_WIDTH), lambda b, j: (b, 0, 0))
    return pl.pallas_call(
        functools.partial(_rglru_kernel, tc=tc),
        grid=(nb, nj),
        in_specs=[pl.BlockSpec((tc, 2 * A_WIDTH), lambda b, j: (b * nj + j, 0)), per_b, per_b,
                  full(cw8), full(cb), full(wa), full(ba), full(wx), full(bx), full(lam)],
        out_specs=[pl.BlockSpec((tc, A_WIDTH), lambda b, j: (b * nj + j, 0)), per_b, per_b],
        out_shape=[jax.ShapeDtypeStruct((m, A_WIDTH), BF16),
                   jax.ShapeDtypeStruct((nb, SUBLANES, A_WIDTH), F32),
                   jax.ShapeDtypeStruct((nb, SUBLANES, A_WIDTH), F32)],
        scratch_shapes=[pltpu.VMEM((SUBLANES + tc, A_WIDTH), F32), pltpu.VMEM((SUBLANES, A_WIDTH), F32)],
        compiler_params=pltpu.CompilerParams(dimension_semantics=("arbitrary", "arbitrary"),
                                             vmem_limit_bytes=VMEM_LIMIT),
        name="rglru",
    )(oa, cs8, h08, cw8, cb, wa, ba, wx, bx, lam)


def _gla_kernel(c_ref, misc_ref, s0_ref, wa2_ref, gba_ref, gn_ref, bdv_ref, y_ref, sout_ref, st_ref, *, ch, nch):
    j = pl.program_id(1)

    @pl.when(j == 0)
    def _():
        st_ref[...] = s0_ref[0]

    row = lax.broadcasted_iota(I32, (ch, ch), 0)
    col = lax.broadcasted_iota(I32, (ch, ch), 1)
    causal = col <= row
    tri = jnp.where(causal, 1.0, 0.0).astype(F32)
    klane = lax.broadcasted_iota(I32, (ch, C_KW), 1)
    vlane = lax.broadcasted_iota(I32, (ch, C_WIDTH), 1)

    for c in range(nch):
        r0 = c * ch
        cq = c_ref[r0:r0 + ch, 0:C_KW]
        ck = c_ref[r0:r0 + ch, C_KW:2 * C_KW]
        cv = c_ref[r0:r0 + ch, 2 * C_KW:2 * C_KW + C_WIDTH]
        cg = c_ref[r0:r0 + ch, 2 * C_KW + C_WIDTH:2 * C_KW + 2 * C_WIDTH]
        z = _dot(misc_ref[r0:r0 + ch, :].astype(BF16), wa2_ref[...]) + gba_ref[...]
        lg = (jnp.minimum(z, 0.0) - jnp.log1p(jnp.exp(-jnp.abs(z)))) * (1.0 / GATE_TAU)
        b = jnp.dot(tri, lg, preferred_element_type=F32, precision=lax.Precision.HIGHEST)
        b_last = b[ch - 1:ch, :]
        qe = (cq * (C_DK ** -0.5) * jnp.exp(b)).astype(BF16)
        ke = (ck * jnp.exp(-b)).astype(BF16)
        kd = (ck * jnp.exp(b_last - b)).astype(BF16)
        vb = cv.astype(BF16)
        st = st_ref[...]
        o = _dot_nt(qe, st.astype(BF16))
        for h in range(C_HEADS):
            qh = jnp.where((klane >= h * C_DK) & (klane < (h + 1) * C_DK), qe, jnp.zeros_like(qe))
            att = jnp.where(causal, _dot_nt(qh, ke), 0.0).astype(BF16)
            oh = _dot(att, vb)
            o = o + jnp.where((vlane >= h * C_DV) & (vlane < (h + 1) * C_DV), oh, 0.0)
        ds = lax.dot_general(vb, kd, (((0,), (0,)), ((), ())), preferred_element_type=F32)
        srow = lax.broadcasted_iota(I32, ds.shape, 0) // C_DV
        scol = lax.broadcasted_iota(I32, ds.shape, 1) // C_DK
        st_ref[...] = st * jnp.exp(b_last) + jnp.where(srow == scol, ds, 0.0)
        on = o * lax.rsqrt(_group_mean_sq(o, bdv_ref[...]) + EPS) * gn_ref[...]
        y_ref[r0:r0 + ch, :] = (on * (cg * jax.nn.sigmoid(cg))).astype(BF16)

    sout_ref[0] = st_ref[...]


def _gla(oc, misc, st0, wa2p, gba, gn, bdv, nb, t, ch, nch):
    m = oc.shape[0]
    tg = ch * nch
    nj = t // tg
    full = lambda a: pl.BlockSpec(a.shape, lambda b, j: (0,) * a.ndim)
    per_b = pl.BlockSpec((1, C_WIDTH, C_KW), lambda b, j: (b, 0, 0))
    return pl.pallas_call(
        functools.partial(_gla_kernel, ch=ch, nch=nch),
        grid=(nb, nj),
        in_specs=[pl.BlockSpec((tg, 768), lambda b, j: (b * nj + j, 0)),
                  pl.BlockSpec((tg, LANES), lambda b, j: (b * nj + j, 0)),
                  per_b, full(wa2p), full(gba), full(gn), full(bdv)],
        out_specs=[pl.BlockSpec((tg, C_WIDTH), lambda b, j: (b * nj + j, 0)), per_b],
        out_shape=[jax.ShapeDtypeStruct((m, C_WIDTH), BF16), jax.ShapeDtypeStruct((nb, C_WIDTH, C_KW), F32)],
        scratch_shapes=[pltpu.VMEM((C_WIDTH, C_KW), F32)],
        compiler_params=pltpu.CompilerParams(dimension_semantics=("arbitrary", "arbitrary"),
                                             vmem_limit_bytes=VMEM_LIMIT),
        name="gla",
    )(oc, misc, st0, wa2p, gba, gn, bdv)


def _radix_select(count_ge, shape, k):
    def bit_body(it, t_u):
        cand_u = t_u | jnp.left_shift(jnp.int32(1), 31 - it)
        tot = count_ge(cand_u ^ INT_MIN)
        return jnp.where(tot >= k, cand_u, t_u)
    t_u = lax.fori_loop(0, 32, bit_body, jnp.zeros(shape, I32))
    return t_u ^ INT_MIN


def _select_mask(key, t_s, need, run_tie, upper):
    gt = key > t_s
    eq = key == t_s
    eqb = jnp.where(eq, 1.0, 0.0).astype(BF16)
    before = _dot(eqb, upper)
    sel = gt | (eq & (run_tie + before < need))
    return sel, run_tie + before[:, LANES - 1:LANES] + jnp.where(eq[:, LANES - 1:LANES], 1.0, 0.0)


def _strict_upper():
    r = lax.broadcasted_iota(I32, (KEY_BLOCK, KEY_BLOCK), 0)
    c = lax.broadcasted_iota(I32, (KEY_BLOCK, KEY_BLOCK), 1)
    return jnp.where(r < c, 1.0, 0.0).astype(BF16)


def _dsa_prompt_kernel(qi_ref, wi_ref, q_ref, ki_ref, k_ref, v_ref, o_ref,
                       key_ref, qix_ref, qx_ref, m_ref, l_ref, acc_ref, *, topk, group):
    i = pl.program_id(1)
    nst = (i + group) // group
    qb = Q_BLOCK
    lane = lax.broadcasted_iota(I32, (qb, LANES), 1)
    qrow = lax.broadcasted_iota(I32, (qb, LANES), 0)
    low_half = lane < HEAD_DIM
    rel = lane - qrow

    def tiles(s):
        for u in range(group):
            kb = s * group + u
            yield kb, pl.multiple_of(kb * KEY_BLOCK, KEY_BLOCK)

    qi = qi_ref[0]
    wi = wi_ref[0]
    zero_b = jnp.zeros((qb, LANES), BF16)
    for h in range(IDX_HEADS):
        blk = qi[:, LANES * (h // 2):LANES * (h // 2 + 1)]
        qix_ref[h * qb:(h + 1) * qb, :] = jnp.where(low_half if h % 2 == 0 else ~low_half, blk, zero_b)
    wib = [jnp.broadcast_to(wi[:, h:h + 1], (qb, LANES)) for h in range(IDX_HEADS)]

    def score_step(s, carry):
        for kb, off in tiles(s):
            d = _dot_nt(qix_ref[...], ki_ref[0, pl.ds(off, KEY_BLOCK), :])
            sc = wib[0] * jnp.maximum(d[0:qb], 0.0)
            for h in range(1, IDX_HEADS):
                sc = sc + wib[h] * jnp.maximum(d[h * qb:(h + 1) * qb], 0.0)
            key_ref[kb] = jnp.where(rel <= (i - kb) * KEY_BLOCK, _float_key(sc), INT_MIN)
        return carry
    lax.fori_loop(0, nst, score_step, 0)

    def count(pred):
        def body(s, c):
            parts = [jnp.where(pred(key_ref[kb]), 1.0, 0.0) for kb, _ in tiles(s)]
            while len(parts) > 1:
                parts = [a + b for a, b in zip(parts[0::2], parts[1::2])] + ([parts[-1]] if len(parts) % 2 else [])
            return c + parts[0]
        c = lax.fori_loop(0, nst, body, jnp.zeros((qb, LANES), F32))
        return jnp.sum(c, axis=1, keepdims=True)

    t_s = _radix_select(lambda cand: count(lambda key: key >= cand), (qb, LANES), float(topk))
    need = float(topk) - count(lambda key: key > t_s)

    q = q_ref[0]
    for g in range(B_KV_HEADS):
        blk = q[:, LANES * g:LANES * (g + 1)]
        want_low = g % 2 == 0
        for h2 in range(2):
            src = blk if (h2 == 0) == want_low else pltpu.roll(blk.astype(F32), HEAD_DIM, 1).astype(BF16)
            qx_ref[g, h2 * qb:(h2 + 1) * qb, :] = jnp.where(low_half if want_low else ~low_half, src, zero_b)
    m_ref[...] = jnp.full(m_ref.shape, NEG_BIG, F32)
    l_ref[...] = jnp.zeros(l_ref.shape, F32)
    acc_ref[...] = jnp.zeros(acc_ref.shape, F32)
    upper = _strict_upper()

    def wide(x):
        return jnp.concatenate([x, x], axis=1)

    def pair_bias(kb):
        b = jnp.concatenate([lax.bitcast_convert_type(key_ref[kb], F32),
                             lax.bitcast_convert_type(key_ref[kb + 1], F32)], axis=1)
        return jnp.concatenate([b, b], axis=0)

    def max_step(s, run_tie):
        for kb, _ in tiles(s):
            sel, run_tie = _select_mask(key_ref[kb], t_s, need, run_tie, upper)
            bias = jnp.where(sel & (rel <= (i - kb) * KEY_BLOCK), 0.0, NEG_BIG)
            key_ref[kb] = lax.bitcast_convert_type(bias, I32)
        mx = [None] * B_KV_HEADS
        for u in range(0, group, 2):
            kb = s * group + u
            bias = pair_bias(kb)
            kblk = k_ref[0, pl.ds(pl.multiple_of(kb * KEY_BLOCK, KEY_BLOCK), 2 * KEY_BLOCK), :]
            for g in range(B_KV_HEADS):
                logit = _dot_nt(qx_ref[g], kblk[:, LANES * (g // 2):LANES * (g // 2 + 1)]) + bias
                top = jnp.maximum(logit[:, :KEY_BLOCK], logit[:, KEY_BLOCK:])
                mx[g] = top if mx[g] is None else jnp.maximum(mx[g], top)
        for g in range(B_KV_HEADS):
            m_ref[g] = jnp.maximum(m_ref[g], mx[g])
        return run_tie
    lax.fori_loop(0, nst, max_step, jnp.zeros((qb, 1), F32))
    for g in range(B_KV_HEADS):
        m_ref[g] = jnp.broadcast_to(jnp.max(m_ref[g], axis=1, keepdims=True), (2 * qb, LANES))

    def sum_step(s, carry):
        lsum = [None] * B_KV_HEADS
        pv = [None] * B_KV_HEADS
        for u in range(0, group, 2):
            kb = s * group + u
            off = pl.multiple_of(kb * KEY_BLOCK, KEY_BLOCK)
            bias = pair_bias(kb)
            kblk = k_ref[0, pl.ds(off, 2 * KEY_BLOCK), :]
            vblk = v_ref[0, pl.ds(off, 2 * KEY_BLOCK), :]
            for g in range(B_KV_HEADS):
                lanes = slice(LANES * (g // 2), LANES * (g // 2 + 1))
                p = jnp.exp2(_dot_nt(qx_ref[g], kblk[:, lanes]) + bias - wide(m_ref[g]))
                ps = p[:, :KEY_BLOCK] + p[:, KEY_BLOCK:]
                d = _dot(p.astype(BF16), vblk[:, lanes])
                lsum[g] = ps if lsum[g] is None else lsum[g] + ps
                pv[g] = d if pv[g] is None else pv[g] + d
        for g in range(B_KV_HEADS):
            l_ref[g] += lsum[g]
            acc_ref[g] += pv[g]
        return carry
    lax.fori_loop(0, nst, sum_step, 0)

    for g in range(B_KV_HEADS):
        og = acc_ref[g] / jnp.sum(l_ref[g], axis=1, keepdims=True)
        want_low = g % 2 == 0
        halves = []
        for h2 in range(2):
            part = og[h2 * qb:(h2 + 1) * qb]
            halves.append(part if (h2 == 0) == want_low else pltpu.roll(part, HEAD_DIM, 1))
        o_ref[0, :, LANES * g:LANES * (g + 1)] = jnp.where(low_half, halves[0], halves[1]).astype(BF16)


def _dsa_prompt(qi, wi, q, ki2, kb16, vb16, nb, t):
    topk = min(TOPK_MAX, t // 4)
    nq = t // Q_BLOCK
    nkb = t // KEY_BLOCK
    group = math.gcd(nkb, 4)
    assert group % 2 == 0, "key tiles are consumed in pairs"
    qspec = lambda n: pl.BlockSpec((1, Q_BLOCK, n), lambda b, i: (b, i, 0))
    kspec = lambda n: pl.BlockSpec((1, t, n), lambda b, i: (b, 0, 0))
    rows = 2 * Q_BLOCK
    return pl.pallas_call(
        functools.partial(_dsa_prompt_kernel, topk=topk, group=group),
        grid=(nb, nq),
        in_specs=[qspec(256), qspec(IDX_HEADS), qspec(B_WIDTH), kspec(LANES), kspec(KV_WIDTH), kspec(KV_WIDTH)],
        out_specs=qspec(B_WIDTH),
        out_shape=jax.ShapeDtypeStruct((nb, t, B_WIDTH), BF16),
        scratch_shapes=[pltpu.VMEM((nkb, Q_BLOCK, KEY_BLOCK), I32),
                        pltpu.VMEM((IDX_HEADS * Q_BLOCK, LANES), BF16),
                        pltpu.VMEM((B_KV_HEADS, rows, LANES), BF16),
                        pltpu.VMEM((B_KV_HEADS, rows, LANES), F32),
                        pltpu.VMEM((B_KV_HEADS, rows, LANES), F32),
                        pltpu.VMEM((B_KV_HEADS, rows, LANES), F32)],
        compiler_params=pltpu.CompilerParams(dimension_semantics=("arbitrary", "arbitrary"),
                                             vmem_limit_bytes=VMEM_LIMIT),
        name="dsa_prompt",
    )(qi, wi, q, ki2, kb16, vb16)


def _page_score_kernel(pt_ref, qi_ref, wi_ref, *refs, npages, t):
    page_refs, s_ref = refs[:npages], refs[npages]
    qi = qi_ref[0]
    wi = wi_ref[0]
    qi_rows = jnp.concatenate([qi[:, IDX_DIM * h:IDX_DIM * (h + 1)] for h in range(IDX_HEADS)], axis=0)
    wib = [jnp.broadcast_to(wi[:, h:h + 1], (t, PAGE_SIZE)) for h in range(IDX_HEADS)]
    for j in range(npages):
        d = _dot(qi_rows, page_refs[j][...].astype(BF16))
        sc = wib[0] * jnp.maximum(d[0:t], 0.0)
        for h in range(1, IDX_HEADS):
            sc = sc + wib[h] * jnp.maximum(d[h * t:(h + 1) * t], 0.0)
        s_ref[0, j] = sc


def _page_scores(page_table, qi, wi, cache_kit, layer, npages):
    nb, npg = page_table.shape
    t = qi.shape[1]
    page = lambda j: pl.BlockSpec((None, None, IDX_DIM, PAGE_SIZE),
                                  lambda b, s, pt: (layer, pt[b, s * npages + j], 0, 0))
    grid_spec = pltpu.PrefetchScalarGridSpec(
        num_scalar_prefetch=1,
        grid=(nb, npg // npages),
        in_specs=[pl.BlockSpec((1, t, 256), lambda b, s, pt: (b, 0, 0)),
                  pl.BlockSpec((1, t, IDX_HEADS), lambda b, s, pt: (b, 0, 0))] + [page(j) for j in range(npages)],
        out_specs=pl.BlockSpec((1, npages, t, PAGE_SIZE), lambda b, s, pt: (b, s, 0, 0)),
    )
    return pl.pallas_call(
        functools.partial(_page_score_kernel, npages=npages, t=t),
        grid_spec=grid_spec,
        out_shape=jax.ShapeDtypeStruct((nb, npg, t, PAGE_SIZE), F32),
        compiler_params=pltpu.CompilerParams(dimension_semantics=("arbitrary", "arbitrary"),
                                             vmem_limit_bytes=VMEM_LIMIT),
        name="page_scores",
    )(page_table, qi, wi, *([cache_kit] * npages))


def _dsa_sample_kernel(pt_ref, s_ref, qi_ref, wi_ref, kin_ref, q_ref, kn_ref, vn_ref, *refs, topk, npg, t, npages):
    kpage_refs, vpage_refs = refs[:npages], refs[npages:2 * npages]
    o_ref, key_ref, thr_ref, need_ref, tie_ref, qx_ref, m_ref, l_ref, acc_ref = refs[2 * npages:]
    step = pl.program_id(1)
    lane = lax.broadcasted_iota(I32, (t, LANES), 1)
    trow = lax.broadcasted_iota(I32, (t, LANES), 0)
    new_ok = lane <= trow
    upper = _strict_upper()

    @pl.when(step == 0)
    def _():
        def key_body(pp, carry):
            key_ref[pp] = _float_key(s_ref[0, pp])
            return carry
        lax.fori_loop(0, npg, key_body, 0)
        qi = qi_ref[0]
        wi = wi_ref[0]
        kin = kin_ref[0]
        s = None
        for h in range(IDX_HEADS):
            d = _dot_nt(qi[:, IDX_DIM * h:IDX_DIM * (h + 1)], kin)
            term = wi[:, h:h + 1] * jnp.maximum(d, 0.0)
            s = term if s is None else s + term
        key_ref[npg] = jnp.where(new_ok, _float_key(s), INT_MIN)

        def count(pred):
            def body(pp, c):
                return c + jnp.where(pred(key_ref[pp]), 1.0, 0.0)
            c = lax.fori_loop(0, npg + 1, body, jnp.zeros((t, LANES), F32))
            return jnp.sum(c, axis=1, keepdims=True)

        t_s = _radix_select(lambda cand: count(lambda key: key >= cand), (t, LANES), float(topk))
        thr_ref[...] = t_s
        need_ref[...] = jnp.broadcast_to(float(topk) - count(lambda key: key > t_s), (t, LANES))
        tie_ref[...] = jnp.zeros(tie_ref.shape, F32)

        q = q_ref[0]
        low_half = lane < HEAD_DIM
        zero_b = jnp.zeros((t, LANES), BF16)
        for h in range(B_HEADS):
            g = h // 2
            blk = q[:, LANES * g:LANES * (g + 1)]
            want_low = g % 2 == 0
            src = blk if (h % 2 == 0) == want_low else pltpu.roll(blk.astype(F32), HEAD_DIM, 1).astype(BF16)
            piece = jnp.where(low_half if want_low else ~low_half, src, zero_b)
            for pb in range(KV_WIDTH // LANES):
                qx_ref[h * t:(h + 1) * t, LANES * pb:LANES * (pb + 1)] = piece if pb == g // 2 else zero_b
        m_ref[...] = jnp.full(m_ref.shape, NEG_BIG, F32)
        l_ref[...] = jnp.zeros(l_ref.shape, F32)
        acc_ref[...] = jnp.zeros(acc_ref.shape, F32)

    def attend(key, kblk, vblk, visible, transposed):
        sel, run_tie = _select_mask(key, thr_ref[...], need_ref[...], tie_ref[:, 0:1], upper)
        tie_ref[...] = jnp.broadcast_to(run_tie, tie_ref.shape)
        if visible is not None:
            sel = sel & visible
        selr = jnp.tile(jnp.where(sel, 1.0, 0.0), (B_HEADS, 1)) > 0.5
        logit = _dot(qx_ref[...], kblk) if transposed else _dot_nt(qx_ref[...], kblk)
        m_old = m_ref[...]
        m_new = jnp.maximum(m_old, jnp.max(jnp.where(selr, logit, NEG_BIG), axis=1, keepdims=True))
        pr = jnp.where(selr, jnp.exp2(logit - m_new), 0.0).astype(BF16)
        alpha = jnp.exp2(m_old - m_new)
        l_ref[...] = alpha * l_ref[...] + jnp.sum(pr.astype(F32), axis=1, keepdims=True)
        pv = _dot_nt(pr, vblk) if transposed else _dot(pr, vblk)
        acc_ref[...] = jnp.tile(alpha, (1, KV_WIDTH // LANES)) * acc_ref[...] + pv
        m_ref[...] = m_new

    for j in range(npages):
        attend(key_ref[step * npages + j], kpage_refs[j][...].astype(BF16), vpage_refs[j][...].astype(BF16),
               None, True)

    @pl.when(step == npg // npages - 1)
    def _():
        attend(key_ref[npg], kn_ref[0], vn_ref[0], new_ok, False)
        o_ref[0] = acc_ref[...] / jnp.tile(l_ref[...], (1, KV_WIDTH // LANES))


def _dsa_sample(page_table, scores, qi, wi, kin_pad, q, kn_pad, vn_pad, cache_kt, cache_vt, layer, npages):
    nb, npg = page_table.shape
    t = qi.shape[1]
    topk = min(TOPK_MAX, (npg * PAGE_SIZE + t) // 4)
    per_b = lambda shape: pl.BlockSpec((1,) + shape, lambda b, s, pt: (b,) + (0,) * len(shape))
    page = lambda j: pl.BlockSpec((None, None, KV_WIDTH, PAGE_SIZE),
                                  lambda b, s, pt: (layer, pt[b, s * npages + j], 0, 0))
    nrows = B_HEADS * t
    grid_spec = pltpu.PrefetchScalarGridSpec(
        num_scalar_prefetch=1,
        grid=(nb, npg // npages),
        in_specs=[per_b((npg, t, PAGE_SIZE)), per_b((t, 256)), per_b((t, IDX_HEADS)), per_b((KEY_BLOCK, IDX_DIM)),
                  per_b((t, B_WIDTH)), per_b((KEY_BLOCK, KV_WIDTH)), per_b((KEY_BLOCK, KV_WIDTH))]
                 + [page(j) for j in range(npages)] + [page(j) for j in range(npages)],
        out_specs=per_b((nrows, KV_WIDTH)),
        scratch_shapes=[pltpu.VMEM((npg + 1, t, LANES), I32), pltpu.VMEM((t, LANES), I32),
                        pltpu.VMEM((t, LANES), F32), pltpu.VMEM((t, LANES), F32),
                        pltpu.VMEM((nrows, KV_WIDTH), BF16), pltpu.VMEM((nrows, LANES), F32),
                        pltpu.VMEM((nrows, LANES), F32), pltpu.VMEM((nrows, KV_WIDTH), F32)],
    )
    return pl.pallas_call(
        functools.partial(_dsa_sample_kernel, topk=topk, npg=npg, t=t, npages=npages),
        grid_spec=grid_spec,
        out_shape=jax.ShapeDtypeStruct((nb, nrows, KV_WIDTH), F32),
        compiler_params=pltpu.CompilerParams(dimension_semantics=("arbitrary", "arbitrary"),
                                             vmem_limit_bytes=VMEM_LIMIT),
        name="dsa_sample",
    )(page_table, scores, qi, wi, kin_pad, q, kn_pad, vn_pad, *([cache_kt] * npages), *([cache_vt] * npages))


def _outmlp_kernel(x_ref, ya_ref, yb_ref, yc_ref, wo_ref, g_ref, wu_ref, wd_ref, o_ref, *, ff_chunk):
    mix = (_dot(ya_ref[...], wo_ref[0:A_WIDTH, :]) + _dot(yb_ref[...], wo_ref[A_WIDTH:A_WIDTH + B_WIDTH, :])
           + _dot(yc_ref[...], wo_ref[A_WIDTH + B_WIDTH:, :]))
    x1 = x_ref[...] + mix
    ms = jnp.mean(x1 * x1, axis=-1, keepdims=True)
    xn = (x1 * lax.rsqrt(ms + EPS) * g_ref[...]).astype(BF16)
    mlp = None
    for c in range(D_FF // ff_chunk):
        hid = jnp.maximum(_dot(xn, wu_ref[:, c * ff_chunk:(c + 1) * ff_chunk]), 0.0)
        down = _dot((hid * hid).astype(BF16), wd_ref[c * ff_chunk:(c + 1) * ff_chunk, :])
        mlp = down if mlp is None else mlp + down
    o_ref[...] = x1 + mlp


def _outmlp(x, ya, yb, yc, wo, g, wu, wd, tm):
    m = x.shape[0]
    row = lambda n: pl.BlockSpec((tm, n), lambda i: (i, 0))
    full = lambda a: pl.BlockSpec(a.shape, lambda i: (0,) * a.ndim)
    return pl.pallas_call(
        functools.partial(_outmlp_kernel, ff_chunk=1024),
        grid=(m // tm,),
        in_specs=[row(D_MODEL), row(A_WIDTH), row(B_WIDTH), row(C_WIDTH), full(wo), full(g), full(wu), full(wd)],
        out_specs=row(D_MODEL),
        out_shape=jax.ShapeDtypeStruct((m, D_MODEL), F32),
        compiler_params=pltpu.CompilerParams(dimension_semantics=("arbitrary",), vmem_limit_bytes=VMEM_LIMIT),
        name="outmlp",
    )(x, ya, yb, yc, wo, g, wu, wd)


def _rope_tables(pos, rows):
    half = ROPE_DIM // 2
    inv_freq = ROPE_THETA ** (-(jnp.arange(half, dtype=F32) * 2.0 / ROPE_DIM))
    ang = pos.astype(F32)[:, None] * inv_freq[None, :]
    cos, sin = jnp.cos(ang), jnp.sin(ang)
    n = pos.shape[0]
    ones = jnp.ones((n, HEAD_DIM - ROPE_DIM), F32)
    zeros = jnp.zeros((n, HEAD_DIM - ROPE_DIM), F32)
    zh = jnp.zeros((n, half), F32)
    c = jnp.concatenate([cos, cos, ones], axis=1)
    sa = jnp.concatenate([-sin, zh, zeros], axis=1)
    sb = jnp.concatenate([zh, sin, zeros], axis=1)
    reps = max(rows // n, 1)
    return tuple(jnp.tile(jnp.concatenate([a, a], axis=1), (reps, 1)) for a in (c, sa, sb))


def _block_diag(blocks):
    n, r, c = blocks.shape
    eye = jnp.eye(n, dtype=blocks.dtype)
    return (blocks[:, :, None, :] * eye[:, None, :, None]).reshape(n * r, n * c)


def _group_mean_matrix(width):
    g = jnp.arange(width) // HEAD_DIM
    return jnp.where(g[:, None] == g[None, :], 1.0 / HEAD_DIM, 0.0).astype(BF16)


def _pad_rows(a, rows):
    return jnp.pad(a, ((0, 0), (0, rows - a.shape[1]), (0, 0)))


def _layer_weights(l, norm_mix, w_in, conv_w, conv_b, rg_wa, rg_ba, rg_wx, rg_bx, rg_lambda, q_norm, k_norm,
                   gla_wa2, gla_ba, gla_norm, w_out, norm_mlp, w_up, w_down):
    wi = w_in[l]
    w_perm = jnp.concatenate(
        [wi[:, 0:1792], wi[:, 1860:2628], wi[:, 1792:1856], wi[:, 2628:2644], wi[:, 1856:1860],
         jnp.zeros((D_MODEL, N_PERM - 2644), wi.dtype)], axis=1).astype(BF16)
    wa2p = jnp.zeros((LANES, C_KW), F32).at[MISC_CA:MISC_CA + GATE_RANK].set(gla_wa2[l]).astype(BF16)
    return dict(
        g_mix=norm_mix[l][None, :], w_perm=w_perm,
        qn=jnp.tile(q_norm[l], B_HEADS)[None, :], kn=jnp.tile(k_norm[l], B_KV_HEADS)[None, :],
        cw8=jnp.pad(conv_w[l], ((0, SUBLANES - CONV_W), (0, 0))), cb=conv_b[l][None, :],
        wa=_block_diag(rg_wa[l]).astype(BF16), ba=rg_ba[l][None, :],
        wx=_block_diag(rg_wx[l]).astype(BF16), bx=rg_bx[l][None, :], lam=rg_lambda[l][None, :],
        wa2p=wa2p, gba=gla_ba[l][None, :], gn=jnp.tile(gla_norm[l], C_HEADS)[None, :],
        wo=w_out[l].astype(BF16), g_mlp=norm_mlp[l][None, :], wu=w_up[l].astype(BF16), wd=w_down[l].astype(BF16),
    )


def _state_to_blockdiag(s):
    nb = s.shape[0]
    eye = jnp.eye(C_HEADS, dtype=s.dtype)
    st = jnp.swapaxes(s, 2, 3)[:, :, :, None, :] * eye[None, :, None, :, None]
    return st.reshape(nb, C_WIDTH, C_KW)


def _blockdiag_to_state(st):
    nb = st.shape[0]
    st = st.reshape(nb, C_HEADS, C_DV, C_HEADS, C_DK)
    s = jnp.stack([st[:, h, :, h, :] for h in range(C_HEADS)], axis=1)
    return jnp.swapaxes(s, 2, 3)


def _trunk_layer(x, nb, t, pos, conv_state, h0, s0, lw, consts, attn_fn, tm, tc, ch, nch):
    bdq, bdk, bdv = consts
    m = nb * t
    tabs = _rope_tables(pos, tm)
    oa, oq, ok, okb, ov, ovb, oqi, omisc, oki2, oc = _inproj(
        x, lw["g_mix"], lw["w_perm"], tabs, lw["qn"], lw["kn"], bdq, bdk, tm)

    cs8 = jnp.pad(conv_state, ((0, 0), (SUBLANES - (CONV_W - 1), 0), (0, 0)))
    h08 = jnp.broadcast_to(h0[:, None, :], (nb, SUBLANES, A_WIDTH))
    ya, cnew, hlast = _rglru(oa, cs8, h08, lw["cw8"], lw["cb"], lw["wa"], lw["ba"], lw["wx"], lw["bx"],
                             lw["lam"], nb, t, tc)

    yc, st = _gla(oc, omisc, _state_to_blockdiag(s0), lw["wa2p"], lw["gba"], lw["gn"], bdv, nb, t, ch, nch)

    wi = omisc[:, MISC_WI:MISC_WI + IDX_HEADS]
    yb = attn_fn(oq, okb, ovb, oqi, oki2, wi)

    y = _outmlp(x, ya, yb, yc, lw["wo"], lw["g_mlp"], lw["wu"], lw["wd"], min(tm, MLP_ROWS))
    states = (ok.reshape(nb, t, B_KV_HEADS, HEAD_DIM), ov.reshape(nb, t, B_KV_HEADS, HEAD_DIM),
              omisc[:, :IDX_DIM].reshape(nb, t, IDX_DIM), cnew[:, SUBLANES - (CONV_W - 1):],
              hlast[:, SUBLANES - 1], _blockdiag_to_state(st))
    return y, states


def kernel(x_prompt, x_sample, cache_k, cache_v, cache_ki, state_conv, state_rglru, state_gla, page_table,
           norm_mix, w_in, conv_w, conv_b, rg_wa, rg_ba, rg_wx, rg_bx, rg_lambda, q_norm, k_norm,
           gla_wa2, gla_ba, gla_norm, w_out, norm_mlp, w_up, w_down):
    bp, tp, _ = x_prompt.shape
    bs, ts, _ = x_sample.shape
    depth = w_in.shape[0]
    npg = page_table.shape[1]
    past = npg * PAGE_SIZE
    assert tp % 512 == 0 and ts == SUBLANES and tp >= CONV_W - 1
    n_pool = cache_k.shape[1]
    cache_kt = jnp.transpose(cache_k, (0, 1, 3, 4, 2)).reshape(depth, n_pool, KV_WIDTH, PAGE_SIZE)
    cache_vt = jnp.transpose(cache_v, (0, 1, 3, 4, 2)).reshape(depth, n_pool, KV_WIDTH, PAGE_SIZE)
    cache_kit = jnp.transpose(cache_ki, (0, 1, 3, 2))
    consts = (_group_mean_matrix(B_WIDTH), _group_mean_matrix(KV_WIDTH), _group_mean_matrix(C_WIDTH))
    pos_p = jnp.arange(tp)
    pos_s = past + jnp.arange(ts)
    dt = x_prompt.dtype
    conv0 = jnp.zeros((bp, CONV_W - 1, A_WIDTH), dt)
    h00 = jnp.zeros((bp, A_WIDTH), dt)
    s00 = jnp.zeros((bp, C_HEADS, C_DK, C_DV), dt)

    yp = x_prompt.reshape(bp * tp, D_MODEL)
    ys = x_sample.reshape(bs * ts, D_MODEL)
    sp, ss = [], []
    for l in range(depth):
        lw = _layer_weights(l, norm_mix, w_in, conv_w, conv_b, rg_wa, rg_ba, rg_wx, rg_bx, rg_lambda, q_norm,
                            k_norm, gla_wa2, gla_ba, gla_norm, w_out, norm_mlp, w_up, w_down)

        def attn_p(oq, okb, ovb, oqi, oki2, wi):
            r3 = lambda a: a.reshape(bp, tp, a.shape[-1])
            return _dsa_prompt(r3(oqi), r3(wi), r3(oq), r3(oki2), r3(okb), r3(ovb), bp, tp).reshape(bp * tp, B_WIDTH)

        def attn_s(oq, okb, ovb, oqi, oki2, wi, l=l):
            r3 = lambda a: a.reshape(bs, ts, a.shape[-1])
            qi3, wi3 = r3(oqi), r3(wi)
            scores = _page_scores(page_table, qi3, wi3, cache_kit, l, math.gcd(npg, SCORE_PAGES))
            kin_pad = _pad_rows(r3(oki2)[:, :, :IDX_DIM], KEY_BLOCK)
            acc = _dsa_sample(page_table, scores, qi3, wi3, kin_pad, r3(oq), _pad_rows(r3(okb), KEY_BLOCK),
                              _pad_rows(r3(ovb), KEY_BLOCK), cache_kt, cache_vt, l, math.gcd(npg, ATTN_PAGES))
            acc = acc.reshape(bs, B_HEADS, ts, B_KV_HEADS, HEAD_DIM)
            o = jnp.stack([acc[:, h, :, h // 2, :] for h in range(B_HEADS)], axis=2)
            return o.reshape(bs * ts, B_WIDTH).astype(BF16)

        yp, st_p = _trunk_layer(yp, bp, tp, pos_p, conv0, h00, s00, lw, consts, attn_p,
                                tm=512, tc=256, ch=math.gcd(tp, CHUNK), nch=512 // math.gcd(tp, CHUNK))
        ys, st_s = _trunk_layer(ys, bs, ts, pos_s, state_conv[l], state_rglru[l], state_gla[l], lw, consts, attn_s,
                                tm=bs * ts, tc=ts, ch=math.gcd(ts, CHUNK), nch=1)
        sp.append(st_p)
        ss.append(st_s)

    outs = [yp.reshape(bp, tp, D_MODEL), ys.reshape(bs, ts, D_MODEL)]
    for states in (sp, ss):
        for idx in range(6):
            outs.append(jnp.stack([s[idx] for s in states]))
    return tuple(outs)
```

```python
import functools
import math

import jax
import jax.numpy as jnp
from jax import lax
from jax.experimental import pallas as pl
from jax.experimental.pallas import tpu as pltpu

F32 = jnp.float32
BF16 = jnp.bfloat16
I32 = jnp.int32
I16 = jnp.int16

D_MODEL = 1024
HEAD_DIM = 64
EPS = 1e-6
A_WIDTH = 256
A_BLOCKS = 4
CONV_W = 4
RG_C = 8.0
B_HEADS = 8
B_KV_HEADS = 4
B_WIDTH = 512
KV_WIDTH = B_KV_HEADS * HEAD_DIM
ROPE_DIM = 16
ROPE_THETA = 500000.0
IDX_HEADS = 4
IDX_DIM = 64
TOPK_MAX = 256
C_WIDTH = 256
C_HEADS = 4
C_DV = 64
C_DK = 32
C_KW = C_HEADS * C_DK
GATE_RANK = 16
GATE_TAU = 16.0
CHUNK = 64
D_FF = 4 * D_MODEL
PAGE_SIZE = 128

LANES = 128
SUBLANES = 8
KEY_BLOCK = 128
Q_BLOCK = 128
SCORE_PAGES = 16
ATTN_PAGES = 16
MLP_ROWS = 256
VMEM_LIMIT = 56 * 1024 * 1024

COL_A = 0
COL_Q = 512
COL_K = 1024
COL_V = 1280
COL_QI = 1536
COL_C = 1792
COL_MISC = 2560
N_PERM = 2688
MISC_CA = 64
MISC_WI = 80

INT_MIN = -(2 ** 31)
HALF_RANGE = 2 ** 15
NEG_BIG = -1e30
LOG2E = math.log2(math.e)
NT_DIMS = (((1,), (1,)), ((), ()))


def _dot(a, b):
    return jnp.dot(a, b, preferred_element_type=F32)


def _dot_nt(a, b):
    return lax.dot_general(a, b, NT_DIMS, preferred_element_type=F32)


def _group_mean_sq(t, bd):
    x2 = t * t
    hi = x2.astype(BF16)
    lo = (x2 - hi.astype(F32)).astype(BF16)
    return _dot(hi, bd) + _dot(lo, bd)


def _rope(t, cos, sin_a, sin_b):
    n = t.shape[1]
    reps = n // LANES
    c = jnp.tile(cos, (1, reps))
    sa = jnp.tile(sin_a, (1, reps))
    sb = jnp.tile(sin_b, (1, reps))
    return t * c + pltpu.roll(t, n - ROPE_DIM // 2, 1) * sa + pltpu.roll(t, ROPE_DIM // 2, 1) * sb


def _float_key(s):
    bits = lax.bitcast_convert_type(s, I32)
    return bits ^ ((bits >> 31) & 0x7FFFFFFF)


def _inproj_kernel(x_ref, g_ref, w_ref, cos_ref, sa_ref, sb_ref, qn_ref, kn_ref, bdq_ref, bdk_ref,
                   oa_ref, oq_ref, ok_ref, okb_ref, ov_ref, ovb_ref, oqi_ref, omisc_ref, oki2_ref, oc_ref):
    x = x_ref[...]
    ms = jnp.mean(x * x, axis=-1, keepdims=True)
    xn = (x * lax.rsqrt(ms + EPS) * g_ref[...]).astype(BF16)

    def proj(lo, hi):
        return _dot(xn, w_ref[:, lo:hi])

    cos = cos_ref[...]
    sa = sa_ref[...]
    sb = sb_ref[...]

    oa_ref[...] = proj(COL_A, COL_Q)
    oc_ref[...] = proj(COL_C, COL_MISC)

    q = proj(COL_Q, COL_K)
    q = q * lax.rsqrt(_group_mean_sq(q, bdq_ref[...]) + EPS) * qn_ref[...]
    q = _rope(q, cos, sa, sb)
    oq_ref[...] = (q * (HEAD_DIM ** -0.5 * LOG2E)).astype(BF16)

    k = proj(COL_K, COL_V)
    k = k * lax.rsqrt(_group_mean_sq(k, bdk_ref[...]) + EPS) * kn_ref[...]
    k = _rope(k, cos, sa, sb)
    ok_ref[...] = k
    okb_ref[...] = k.astype(BF16)

    v = proj(COL_V, COL_QI)
    ov_ref[...] = v
    ovb_ref[...] = v.astype(BF16)

    qi = _rope(proj(COL_QI, COL_C), cos, sa, sb)
    oqi_ref[...] = (qi * (IDX_DIM ** -0.5)).astype(BF16)

    misc = proj(COL_MISC, N_PERM)
    lane = lax.broadcasted_iota(I32, misc.shape, 1)
    is_ki = lane < IDX_DIM
    misc = _rope(misc, jnp.where(is_ki, cos, 1.0), jnp.where(is_ki, sa, 0.0), jnp.where(is_ki, sb, 0.0))
    is_wi = (lane >= MISC_WI) & (lane < MISC_WI + IDX_HEADS)
    misc = jnp.where(is_wi, misc * (IDX_HEADS ** -0.5), misc)
    omisc_ref[...] = misc
    oki2_ref[...] = jnp.where(is_ki, misc, pltpu.roll(misc, IDX_DIM, 1)).astype(BF16)


def _inproj(x, g, w_perm, tabs, qn, kn, bdq, bdk, tm):
    m = x.shape[0]
    cos, sa, sb = tabs
    nper = cos.shape[0] // tm
    row = lambda n: pl.BlockSpec((tm, n), lambda i: (i, 0))
    full = lambda a: pl.BlockSpec(a.shape, lambda i: (0,) * a.ndim)
    tab = pl.BlockSpec((tm, LANES), lambda i: (i % nper, 0))
    widths = [(512, F32), (512, BF16), (KV_WIDTH, F32), (KV_WIDTH, BF16), (KV_WIDTH, F32), (KV_WIDTH, BF16),
              (256, BF16), (LANES, F32), (LANES, BF16), (768, F32)]
    return pl.pallas_call(
        _inproj_kernel,
        grid=(m // tm,),
        in_specs=[row(D_MODEL), full(g), full(w_perm), tab, tab, tab, full(qn), full(kn), full(bdq), full(bdk)],
        out_specs=[row(n) for n, _ in widths],
        out_shape=[jax.ShapeDtypeStruct((m, n), dt) for n, dt in widths],
        compiler_params=pltpu.CompilerParams(dimension_semantics=("arbitrary",), vmem_limit_bytes=VMEM_LIMIT),
        name="inproj",
    )(x, g, w_perm, cos, sa, sb, qn, kn, bdq, bdk)


def _rglru_kernel(a_ref, cs_ref, h0_ref, cw_ref, cb_ref, wa_ref, ba_ref, wx_ref, bx_ref, lam_ref,
                  y_ref, cnew_ref, hlast_ref, ext_ref, hcar_ref, *, tc):
    j = pl.program_id(1)

    @pl.when(j == 0)
    def _():
        ext_ref[0:SUBLANES, :] = cs_ref[0]
        hcar_ref[...] = h0_ref[0]

    ag = a_ref[:, 0:A_WIDTH]
    ext_ref[SUBLANES:SUBLANES + tc, :] = a_ref[:, A_WIDTH:2 * A_WIDTH]
    u = cb_ref[...]
    for jj in range(CONV_W):
        lo = SUBLANES - (CONV_W - 1) + jj
        u = u + ext_ref[lo:lo + tc, :] * cw_ref[jj:jj + 1, :]
    tail = ext_ref[tc:tc + SUBLANES, :]
    ext_ref[0:SUBLANES, :] = tail
    cnew_ref[0] = tail

    ub = u.astype(BF16)
    r = jax.nn.sigmoid(_dot(ub, wa_ref[...]) + ba_ref[...])
    gi = jax.nn.sigmoid(_dot(ub, wx_ref[...]) + bx_ref[...])
    nl = -lam_ref[...]
    softplus = jnp.maximum(nl, 0.0) + jnp.log1p(jnp.exp(-jnp.abs(nl)))
    log_a = -RG_C * r * softplus
    a = jnp.exp(log_a)
    b = jnp.sqrt(-jnp.tanh(log_a) * (a * a + 1.0)) * gi * u

    row = lax.broadcasted_iota(I32, a.shape, 0)
    s = 1
    while s < tc:
        keep = row >= s
        a_s = jnp.where(keep, pltpu.roll(a, s, 0), 1.0)
        b_s = jnp.where(keep, pltpu.roll(b, s, 0), 0.0)
        b = a * b_s + b
        a = a * a_s
        s *= 2
    h = a * hcar_ref[SUBLANES - 1:SUBLANES, :] + b
    htail = h[tc - SUBLANES:tc, :]
    hcar_ref[...] = htail
    hlast_ref[0] = htail
    y_ref[...] = (jax.nn.gelu(ag) * h).astype(BF16)


def _rglru(oa, cs8, h08, cw8, cb, wa, ba, wx, bx, lam, nb, t, tc):
    m = oa.shape[0]
    nj = t // tc
    full = lambda a: pl.BlockSpec(a.shape, lambda b, j: (0,) * a.ndim)
    per_b = pl.BlockSpec((1, SUBLANES, A_WIDTH), lambda b, j: (b, 0, 0))
    return pl.pallas_call(
        functools.partial(_rglru_kernel, tc=tc),
        grid=(nb, nj),
        in_specs=[pl.BlockSpec((tc, 2 * A_WIDTH), lambda b, j: (b * nj + j, 0)), per_b, per_b,
                  full(cw8), full(cb), full(wa), full(ba), full(wx), full(bx), full(lam)],
        out_specs=[pl.BlockSpec((tc, A_WIDTH), lambda b, j: (b * nj + j, 0)), per_b, per_b],
        out_shape=[jax.ShapeDtypeStruct((m, A_WIDTH), BF16),
                   jax.ShapeDtypeStruct((nb, SUBLANES, A_WIDTH), F32),
                   jax.ShapeDtypeStruct((nb, SUBLANES, A_WIDTH), F32)],
        scratch_shapes=[pltpu.VMEM((SUBLANES + tc, A_WIDTH), F32), pltpu.VMEM((SUBLANES, A_WIDTH), F32)],
        compiler_params=pltpu.CompilerParams(dimension_semantics=("arbitrary", "arbitrary"),
                                             vmem_limit_bytes=VMEM_LIMIT),
        name="rglru",
    )(oa, cs8, h08, cw8, cb, wa, ba, wx, bx, lam)


def _gla_kernel(c_ref, misc_ref, s0_ref, wa2_ref, gba_ref, gn_ref, bdv_ref, y_ref, sout_ref, st_ref, *, ch, nch):
    j = pl.program_id(1)

    @pl.when(j == 0)
    def _():
        st_ref[...] = s0_ref[0]

    row = lax.broadcasted_iota(I32, (ch, ch), 0)
    col = lax.broadcasted_iota(I32, (ch, ch), 1)
    causal = col <= row
    tri = jnp.where(causal, 1.0, 0.0).astype(F32)
    klane = lax.broadcasted_iota(I32, (ch, C_KW), 1)
    vlane = lax.broadcasted_iota(I32, (ch, C_WIDTH), 1)

    for c in range(nch):
        r0 = c * ch
        cq = c_ref[r0:r0 + ch, 0:C_KW]
        ck = c_ref[r0:r0 + ch, C_KW:2 * C_KW]
        cv = c_ref[r0:r0 + ch, 2 * C_KW:2 * C_KW + C_WIDTH]
        cg = c_ref[r0:r0 + ch, 2 * C_KW + C_WIDTH:2 * C_KW + 2 * C_WIDTH]
        z = _dot(misc_ref[r0:r0 + ch, :].astype(BF16), wa2_ref[...]) + gba_ref[...]
        lg = (jnp.minimum(z, 0.0) - jnp.log1p(jnp.exp(-jnp.abs(z)))) * (1.0 / GATE_TAU)
        b = jnp.dot(tri, lg, preferred_element_type=F32, precision=lax.Precision.HIGHEST)
        b_last = b[ch - 1:ch, :]
        qe = (cq * (C_DK ** -0.5) * jnp.exp(b)).astype(BF16)
        ke = (ck * jnp.exp(-b)).astype(BF16)
        kd = (ck * jnp.exp(b_last - b)).astype(BF16)
        vb = cv.astype(BF16)
        st = st_ref[...]
        o = _dot_nt(qe, st.astype(BF16))
        for h in range(C_HEADS):
            qh = jnp.where((klane >= h * C_DK) & (klane < (h + 1) * C_DK), qe, jnp.zeros_like(qe))
            att = jnp.where(causal, _dot_nt(qh, ke), 0.0).astype(BF16)
            oh = _dot(att, vb)
            o = o + jnp.where((vlane >= h * C_DV) & (vlane < (h + 1) * C_DV), oh, 0.0)
        ds = lax.dot_general(vb, kd, (((0,), (0,)), ((), ())), preferred_element_type=F32)
        srow = lax.broadcasted_iota(I32, ds.shape, 0) // C_DV
        scol = lax.broadcasted_iota(I32, ds.shape, 1) // C_DK
        st_ref[...] = st * jnp.exp(b_last) + jnp.where(srow == scol, ds, 0.0)
        on = o * lax.rsqrt(_group_mean_sq(o, bdv_ref[...]) + EPS) * gn_ref[...]
        y_ref[r0:r0 + ch, :] = (on * (cg * jax.nn.sigmoid(cg))).astype(BF16)

    sout_ref[0] = st_ref[...]


def _gla(oc, misc, st0, wa2p, gba, gn, bdv, nb, t, ch, nch):
    m = oc.shape[0]
    tg = ch * nch
    nj = t // tg
    full = lambda a: pl.BlockSpec(a.shape, lambda b, j: (0,) * a.ndim)
    per_b = pl.BlockSpec((1, C_WIDTH, C_KW), lambda b, j: (b, 0, 0))
    return pl.pallas_call(
        functools.partial(_gla_kernel, ch=ch, nch=nch),
        grid=(nb, nj),
        in_specs=[pl.BlockSpec((tg, 768), lambda b, j: (b * nj + j, 0)),
                  pl.BlockSpec((tg, LANES), lambda b, j: (b * nj + j, 0)),
                  per_b, full(wa2p), full(gba), full(gn), full(bdv)],
        out_specs=[pl.BlockSpec((tg, C_WIDTH), lambda b, j: (b * nj + j, 0)), per_b],
        out_shape=[jax.ShapeDtypeStruct((m, C_WIDTH), BF16), jax.ShapeDtypeStruct((nb, C_WIDTH, C_KW), F32)],
        scratch_shapes=[pltpu.VMEM((C_WIDTH, C_KW), F32)],
        compiler_params=pltpu.CompilerParams(dimension_semantics=("arbitrary", "arbitrary"),
                                             vmem_limit_bytes=VMEM_LIMIT),
        name="gla",
    )(oc, misc, st0, wa2p, gba, gn, bdv)


def _radix_select_bits(count_ge, shape, k, nbits):
    def bit_body(it, t_u):
        cand_u = t_u | jnp.left_shift(jnp.int32(1), nbits - 1 - it)
        return jnp.where(count_ge(cand_u) >= k, cand_u, t_u)
    return lax.fori_loop(0, nbits, bit_body, jnp.zeros(shape, I32))


def _radix_select(count_ge, shape, k):
    return _radix_select_bits(lambda cand_u: count_ge(cand_u ^ INT_MIN), shape, k, 32) ^ INT_MIN


def _select_masks(keys, t_s, need, run_tie, upper):
    eqs = [key == t_s for key in keys]
    befores = [_dot(jnp.where(eq, 1.0, 0.0).astype(BF16), upper) for eq in eqs]
    sels = []
    for key, eq, before in zip(keys, eqs, befores):
        sels.append((key > t_s) | (eq & (run_tie + before < need)))
        run_tie = run_tie + before[:, LANES - 1:LANES] + jnp.where(eq[:, LANES - 1:LANES], 1.0, 0.0)
    return sels, run_tie


def _strict_upper():
    r = lax.broadcasted_iota(I32, (KEY_BLOCK, KEY_BLOCK), 0)
    c = lax.broadcasted_iota(I32, (KEY_BLOCK, KEY_BLOCK), 1)
    return jnp.where(r < c, 1.0, 0.0).astype(BF16)


def _dsa_prompt_kernel(qi_ref, wi_ref, q_ref, ki_ref, k_ref, v_ref, o_ref,
                       key_ref, hi_ref, lo_ref, qix_ref, qx_ref, m_ref, l_ref, acc_ref, *, topk, group):
    i = pl.program_id(1)
    nst = (i + group) // group
    qb = Q_BLOCK
    lane = lax.broadcasted_iota(I32, (qb, LANES), 1)
    qrow = lax.broadcasted_iota(I32, (qb, LANES), 0)
    low_half = lane < HEAD_DIM
    rel = lane - qrow

    def tiles(s):
        for u in range(group):
            kb = s * group + u
            yield kb, pl.multiple_of(kb * KEY_BLOCK, KEY_BLOCK)

    qi = qi_ref[0]
    wi = wi_ref[0]
    zero_b = jnp.zeros((qb, LANES), BF16)
    for h in range(IDX_HEADS):
        blk = qi[:, LANES * (h // 2):LANES * (h // 2 + 1)]
        qix_ref[h * qb:(h + 1) * qb, :] = jnp.where(low_half if h % 2 == 0 else ~low_half, blk, zero_b)
    wib = [jnp.broadcast_to(wi[:, h:h + 1], (qb, LANES)) for h in range(IDX_HEADS)]

    def score_step(s, carry):
        for kb, off in tiles(s):
            d = _dot_nt(qix_ref[...], ki_ref[0, pl.ds(off, KEY_BLOCK), :])
            sc = wib[0] * jnp.maximum(d[0:qb], 0.0)
            for h in range(1, IDX_HEADS):
                sc = sc + wib[h] * jnp.maximum(d[h * qb:(h + 1) * qb], 0.0)
            key = jnp.where(rel <= (i - kb) * KEY_BLOCK, _float_key(sc), INT_MIN)
            key_ref[kb] = key
            hi_ref[kb] = (key >> 16).astype(I16)
            lo_ref[kb] = ((key & 0xFFFF) - HALF_RANGE).astype(I16)
        return carry
    lax.fori_loop(0, nst, score_step, 0)

    def tree_sum(parts):
        while len(parts) > 1:
            parts = [a + b for a, b in zip(parts[0::2], parts[1::2])] + ([parts[-1]] if len(parts) % 2 else [])
        return parts[0]

    def count(ref, pred, one, zero):
        def body(s, c):
            return c + tree_sum([jnp.where(pred(ref[kb]), one, zero) for kb, _ in tiles(s)])
        c = lax.fori_loop(0, nst, body, jnp.zeros((qb, LANES), one.dtype))
        return jnp.sum(c.astype(F32), axis=1, keepdims=True)

    def count16(ref, pred):
        return count(ref, pred, jnp.ones((qb, LANES), I16), jnp.zeros((qb, LANES), I16))

    def to16(cand_u):
        return (cand_u - HALF_RANGE).astype(I16)

    hi_t = to16(_radix_select_bits(lambda c: count16(hi_ref, lambda h: h >= to16(c)), (qb, LANES), float(topk), 16))
    rest = float(topk) - count16(hi_ref, lambda h: h > hi_t)

    def band_step(s, carry):
        for kb, _ in tiles(s):
            lo_ref[kb] = jnp.where(hi_ref[kb] == hi_t, lo_ref[kb], jnp.full((qb, LANES), -HALF_RANGE, I16))
        return carry
    lax.fori_loop(0, nst, band_step, 0)
    lo_u = _radix_select_bits(lambda c: count16(lo_ref, lambda v: v >= to16(c)), (qb, LANES), rest, 16)
    t_s = (hi_t.astype(I32) << 16) | lo_u
    need = float(topk) - count(key_ref, lambda key: key > t_s, jnp.ones((qb, LANES), F32), jnp.zeros((qb, LANES), F32))

    q = q_ref[0]
    for g in range(B_KV_HEADS):
        blk = q[:, LANES * g:LANES * (g + 1)]
        want_low = g % 2 == 0
        for h2 in range(2):
            src = blk if (h2 == 0) == want_low else pltpu.roll(blk.astype(F32), HEAD_DIM, 1).astype(BF16)
            qx_ref[g, h2 * qb:(h2 + 1) * qb, :] = jnp.where(low_half if want_low else ~low_half, src, zero_b)
    m_ref[...] = jnp.full(m_ref.shape, NEG_BIG, F32)
    l_ref[...] = jnp.zeros(l_ref.shape, F32)
    acc_ref[...] = jnp.zeros(acc_ref.shape, F32)
    upper = _strict_upper()

    def wide(x):
        return jnp.concatenate([x, x], axis=1)

    def pair_bias(kb):
        b = jnp.concatenate([lax.bitcast_convert_type(key_ref[kb], F32),
                             lax.bitcast_convert_type(key_ref[kb + 1], F32)], axis=1)
        return jnp.concatenate([b, b], axis=0)

    def pairs(s):
        for u in range(0, group, 2):
            kb = s * group + u
            yield kb, pl.ds(pl.multiple_of(kb * KEY_BLOCK, KEY_BLOCK), 2 * KEY_BLOCK)

    def max_step(s, run_tie):
        kbs = [kb for kb, _ in tiles(s)]
        sels, run_tie = _select_masks([key_ref[kb] for kb in kbs], t_s, need, run_tie, upper)
        for kb, sel in zip(kbs, sels):
            bias = jnp.where(sel & (rel <= (i - kb) * KEY_BLOCK), 0.0, NEG_BIG)
            key_ref[kb] = lax.bitcast_convert_type(bias, I32)
        for g in range(B_KV_HEADS):
            lanes = slice(LANES * (g // 2), LANES * (g // 2 + 1))
            mx = m_ref[g]
            for kb, rows in pairs(s):
                logit = _dot_nt(qx_ref[g], k_ref[0, rows, lanes]) + pair_bias(kb)
                mx = jnp.maximum(mx, jnp.maximum(logit[:, :KEY_BLOCK], logit[:, KEY_BLOCK:]))
            m_ref[g] = mx
        return run_tie
    lax.fori_loop(0, nst, max_step, jnp.zeros((qb, 1), F32))
    for g in range(B_KV_HEADS):
        m_ref[g] = jnp.broadcast_to(jnp.max(m_ref[g], axis=1, keepdims=True), (2 * qb, LANES))

    def sum_step(s, carry):
        for g in range(B_KV_HEADS):
            lanes = slice(LANES * (g // 2), LANES * (g // 2 + 1))
            m = wide(m_ref[g])
            lsum = l_ref[g]
            pv = None
            for kb, rows in pairs(s):
                p = jnp.exp2(_dot_nt(qx_ref[g], k_ref[0, rows, lanes]) + pair_bias(kb) - m)
                lsum = lsum + (p[:, :KEY_BLOCK] + p[:, KEY_BLOCK:])
                d = _dot(p.astype(BF16), v_ref[0, rows, lanes])
                pv = d if pv is None else pv + d
            l_ref[g] = lsum
            acc_ref[g] += pv
        return carry
    lax.fori_loop(0, nst, sum_step, 0)

    for g in range(B_KV_HEADS):
        og = acc_ref[g] / jnp.sum(l_ref[g], axis=1, keepdims=True)
        want_low = g % 2 == 0
        halves = []
        for h2 in range(2):
            part = og[h2 * qb:(h2 + 1) * qb]
            halves.append(part if (h2 == 0) == want_low else pltpu.roll(part, HEAD_DIM, 1))
        o_ref[0, :, LANES * g:LANES * (g + 1)] = jnp.where(low_half, halves[0], halves[1]).astype(BF16)


def _dsa_prompt(qi, wi, q, ki2, kb16, vb16, nb, t):
    topk = min(TOPK_MAX, t // 4)
    nq = t // Q_BLOCK
    nkb = t // KEY_BLOCK
    group = math.gcd(nkb, 4)
    assert group % 2 == 0, "key tiles are consumed in pairs"
    qspec = lambda n: pl.BlockSpec((1, Q_BLOCK, n), lambda b, i: (b, i, 0))
    kspec = lambda n: pl.BlockSpec((1, t, n), lambda b, i: (b, 0, 0))
    rows = 2 * Q_BLOCK
    return pl.pallas_call(
        functools.partial(_dsa_prompt_kernel, topk=topk, group=group),
        grid=(nb, nq),
        in_specs=[qspec(256), qspec(IDX_HEADS), qspec(B_WIDTH), kspec(LANES), kspec(KV_WIDTH), kspec(KV_WIDTH)],
        out_specs=qspec(B_WIDTH),
        out_shape=jax.ShapeDtypeStruct((nb, t, B_WIDTH), BF16),
        scratch_shapes=[pltpu.VMEM((nkb, Q_BLOCK, KEY_BLOCK), I32),
                        pltpu.VMEM((nkb, Q_BLOCK, KEY_BLOCK), I16),
                        pltpu.VMEM((nkb, Q_BLOCK, KEY_BLOCK), I16),
                        pltpu.VMEM((IDX_HEADS * Q_BLOCK, LANES), BF16),
                        pltpu.VMEM((B_KV_HEADS, rows, LANES), BF16),
                        pltpu.VMEM((B_KV_HEADS, rows, LANES), F32),
                        pltpu.VMEM((B_KV_HEADS, rows, LANES), F32),
                        pltpu.VMEM((B_KV_HEADS, rows, LANES), F32)],
        compiler_params=pltpu.CompilerParams(dimension_semantics=("arbitrary", "arbitrary"),
                                             vmem_limit_bytes=VMEM_LIMIT),
        name="dsa_prompt",
    )(qi, wi, q, ki2, kb16, vb16)


def _page_score_kernel(pt_ref, qi_ref, wi_ref, *refs, npages, t):
    page_refs, s_ref = refs[:npages], refs[npages]
    qi = qi_ref[0]
    wi = wi_ref[0]
    qi_rows = jnp.concatenate([qi[:, IDX_DIM * h:IDX_DIM * (h + 1)] for h in range(IDX_HEADS)], axis=0)
    wib = [jnp.broadcast_to(wi[:, h:h + 1], (t, PAGE_SIZE)) for h in range(IDX_HEADS)]
    for j in range(npages):
        d = _dot(qi_rows, page_refs[j][...].astype(BF16))
        sc = wib[0] * jnp.maximum(d[0:t], 0.0)
        for h in range(1, IDX_HEADS):
            sc = sc + wib[h] * jnp.maximum(d[h * t:(h + 1) * t], 0.0)
        s_ref[0, j] = sc


def _page_scores(page_table, qi, wi, cache_kit, layer, npages):
    nb, npg = page_table.shape
    t = qi.shape[1]
    page = lambda j: pl.BlockSpec((None, None, IDX_DIM, PAGE_SIZE),
                                  lambda b, s, pt: (layer, pt[b, s * npages + j], 0, 0))
    grid_spec = pltpu.PrefetchScalarGridSpec(
        num_scalar_prefetch=1,
        grid=(nb, npg // npages),
        in_specs=[pl.BlockSpec((1, t, 256), lambda b, s, pt: (b, 0, 0)),
                  pl.BlockSpec((1, t, IDX_HEADS), lambda b, s, pt: (b, 0, 0))] + [page(j) for j in range(npages)],
        out_specs=pl.BlockSpec((1, npages, t, PAGE_SIZE), lambda b, s, pt: (b, s, 0, 0)),
    )
    return pl.pallas_call(
        functools.partial(_page_score_kernel, npages=npages, t=t),
        grid_spec=grid_spec,
        out_shape=jax.ShapeDtypeStruct((nb, npg, t, PAGE_SIZE), F32),
        compiler_params=pltpu.CompilerParams(dimension_semantics=("arbitrary", "arbitrary"),
                                             vmem_limit_bytes=VMEM_LIMIT),
        name="page_scores",
    )(page_table, qi, wi, *([cache_kit] * npages))


def _dsa_sample_kernel(pt_ref, s_ref, qi_ref, wi_ref, kin_ref, q_ref, kn_ref, vn_ref, *refs, topk, npg, t, npages):
    kpage_refs, vpage_refs = refs[:npages], refs[npages:2 * npages]
    o_ref, key_ref, thr_ref, need_ref, tie_ref, qx_ref, m_ref, l_ref, acc_ref = refs[2 * npages:]
    step = pl.program_id(1)
    lane = lax.broadcasted_iota(I32, (t, LANES), 1)
    trow = lax.broadcasted_iota(I32, (t, LANES), 0)
    new_ok = lane <= trow
    upper = _strict_upper()

    @pl.when(step == 0)
    def _():
        def key_body(pp, carry):
            key_ref[pp] = _float_key(s_ref[0, pp])
            return carry
        lax.fori_loop(0, npg, key_body, 0)
        qi = qi_ref[0]
        wi = wi_ref[0]
        kin = kin_ref[0]
        s = None
        for h in range(IDX_HEADS):
            d = _dot_nt(qi[:, IDX_DIM * h:IDX_DIM * (h + 1)], kin)
            term = wi[:, h:h + 1] * jnp.maximum(d, 0.0)
            s = term if s is None else s + term
        key_ref[npg] = jnp.where(new_ok, _float_key(s), INT_MIN)

        def count(pred):
            def body(pp, c):
                return c + jnp.where(pred(key_ref[pp]), 1.0, 0.0)
            c = lax.fori_loop(0, npg + 1, body, jnp.zeros((t, LANES), F32))
            return jnp.sum(c, axis=1, keepdims=True)

        t_s = _radix_select(lambda cand: count(lambda key: key >= cand), (t, LANES), float(topk))
        thr_ref[...] = t_s
        need_ref[...] = jnp.broadcast_to(float(topk) - count(lambda key: key > t_s), (t, LANES))
        tie_ref[...] = jnp.zeros(tie_ref.shape, F32)

        q = q_ref[0]
        low_half = lane < HEAD_DIM
        zero_b = jnp.zeros((t, LANES), BF16)
        for h in range(B_HEADS):
            g = h // 2
            blk = q[:, LANES * g:LANES * (g + 1)]
            want_low = g % 2 == 0
            src = blk if (h % 2 == 0) == want_low else pltpu.roll(blk.astype(F32), HEAD_DIM, 1).astype(BF16)
            piece = jnp.where(low_half if want_low else ~low_half, src, zero_b)
            for pb in range(KV_WIDTH // LANES):
                qx_ref[h * t:(h + 1) * t, LANES * pb:LANES * (pb + 1)] = piece if pb == g // 2 else zero_b
        m_ref[...] = jnp.full(m_ref.shape, NEG_BIG, F32)
        l_ref[...] = jnp.zeros(l_ref.shape, F32)
        acc_ref[...] = jnp.zeros(acc_ref.shape, F32)

    def attend(keys, kblks, vblks, visible, transposed):
        n = len(keys)
        sels, run_tie = _select_masks(keys, thr_ref[...], need_ref[...], tie_ref[:, 0:1], upper)
        tie_ref[...] = jnp.broadcast_to(run_tie, tie_ref.shape)
        sel = jnp.concatenate(sels, axis=1)
        if visible is not None:
            sel = sel & visible
        selr = jnp.tile(jnp.where(sel, 1.0, 0.0), (B_HEADS, 1)) > 0.5
        qx = qx_ref[...]
        logit = jnp.concatenate([_dot(qx, kb) if transposed else _dot_nt(qx, kb) for kb in kblks], axis=1)
        m_old = m_ref[...]
        m_new = jnp.maximum(m_old, jnp.max(jnp.where(selr, logit, NEG_BIG), axis=1, keepdims=True))
        pr = jnp.where(selr, jnp.exp2(logit - jnp.tile(m_new, (1, n))), 0.0).astype(BF16)
        alpha = jnp.exp2(m_old - m_new)
        l_ref[...] = alpha * l_ref[...] + jnp.sum(pr.astype(F32), axis=1, keepdims=True)
        pv = None
        for j, vb in enumerate(vblks):
            pj = pr[:, KEY_BLOCK * j:KEY_BLOCK * (j + 1)]
            d = _dot_nt(pj, vb) if transposed else _dot(pj, vb)
            pv = d if pv is None else pv + d
        acc_ref[...] = jnp.tile(alpha, (1, KV_WIDTH // LANES)) * acc_ref[...] + pv
        m_ref[...] = m_new

    attend([key_ref[step * npages + j] for j in range(npages)], [r[...].astype(BF16) for r in kpage_refs],
           [r[...].astype(BF16) for r in vpage_refs], None, True)

    @pl.when(step == npg // npages - 1)
    def _():
        attend([key_ref[npg]], [kn_ref[0]], [vn_ref[0]], new_ok, False)
        o_ref[0] = acc_ref[...] / jnp.tile(l_ref[...], (1, KV_WIDTH // LANES))


def _dsa_sample(page_table, scores, qi, wi, kin_pad, q, kn_pad, vn_pad, cache_kt, cache_vt, layer, npages):
    nb, npg = page_table.shape
    t = qi.shape[1]
    topk = min(TOPK_MAX, (npg * PAGE_SIZE + t) // 4)
    per_b = lambda shape: pl.BlockSpec((1,) + shape, lambda b, s, pt: (b,) + (0,) * len(shape))
    page = lambda j: pl.BlockSpec((None, None, KV_WIDTH, PAGE_SIZE),
                                  lambda b, s, pt: (layer, pt[b, s * npages + j], 0, 0))
    nrows = B_HEADS * t
    grid_spec = pltpu.PrefetchScalarGridSpec(
        num_scalar_prefetch=1,
        grid=(nb, npg // npages),
        in_specs=[per_b((npg, t, PAGE_SIZE)), per_b((t, 256)), per_b((t, IDX_HEADS)), per_b((KEY_BLOCK, IDX_DIM)),
                  per_b((t, B_WIDTH)), per_b((KEY_BLOCK, KV_WIDTH)), per_b((KEY_BLOCK, KV_WIDTH))]
                 + [page(j) for j in range(npages)] + [page(j) for j in range(npages)],
        out_specs=per_b((nrows, KV_WIDTH)),
        scratch_shapes=[pltpu.VMEM((npg + 1, t, LANES), I32), pltpu.VMEM((t, LANES), I32),
                        pltpu.VMEM((t, LANES), F32), pltpu.VMEM((t, LANES), F32),
                        pltpu.VMEM((nrows, KV_WIDTH), BF16), pltpu.VMEM((nrows, LANES), F32),
                        pltpu.VMEM((nrows, LANES), F32), pltpu.VMEM((nrows, KV_WIDTH), F32)],
    )
    return pl.pallas_call(
        functools.partial(_dsa_sample_kernel, topk=topk, npg=npg, t=t, npages=npages),
        grid_spec=grid_spec,
        out_shape=jax.ShapeDtypeStruct((nb, nrows, KV_WIDTH), F32),
        compiler_params=pltpu.CompilerParams(dimension_semantics=("arbitrary", "arbitrary"),
                                             vmem_limit_bytes=VMEM_LIMIT),
        name="dsa_sample",
    )(page_table, scores, qi, wi, kin_pad, q, kn_pad, vn_pad, *([cache_kt] * npages), *([cache_vt] * npages))


def _outmlp_kernel(x_ref, ya_ref, yb_ref, yc_ref, wo_ref, g_ref, wu_ref, wd_ref, o_ref, *, ff_chunk):
    mix = (_dot(ya_ref[...], wo_ref[0:A_WIDTH, :]) + _dot(yb_ref[...], wo_ref[A_WIDTH:A_WIDTH + B_WIDTH, :])
           + _dot(yc_ref[...], wo_ref[A_WIDTH + B_WIDTH:, :]))
    x1 = x_ref[...] + mix
    ms = jnp.mean(x1 * x1, axis=-1, keepdims=True)
    xn = (x1 * lax.rsqrt(ms + EPS) * g_ref[...]).astype(BF16)
    mlp = None
    for c in range(D_FF // ff_chunk):
        hid = jnp.maximum(_dot(xn, wu_ref[:, c * ff_chunk:(c + 1) * ff_chunk]), 0.0)
        down = _dot((hid * hid).astype(BF16), wd_ref[c * ff_chunk:(c + 1) * ff_chunk, :])
        mlp = down if mlp is None else mlp + down
    o_ref[...] = x1 + mlp


def _outmlp(x, ya, yb, yc, wo, g, wu, wd, tm):
    m = x.shape[0]
    row = lambda n: pl.BlockSpec((tm, n), lambda i: (i, 0))
    full = lambda a: pl.BlockSpec(a.shape, lambda i: (0,) * a.ndim)
    return pl.pallas_call(
        functools.partial(_outmlp_kernel, ff_chunk=1024),
        grid=(m // tm,),
        in_specs=[row(D_MODEL), row(A_WIDTH), row(B_WIDTH), row(C_WIDTH), full(wo), full(g), full(wu), full(wd)],
        out_specs=row(D_MODEL),
        out_shape=jax.ShapeDtypeStruct((m, D_MODEL), F32),
        compiler_params=pltpu.CompilerParams(dimension_semantics=("arbitrary",), vmem_limit_bytes=VMEM_LIMIT),
        name="outmlp",
    )(x, ya, yb, yc, wo, g, wu, wd)


def _rope_tables(pos, rows):
    half = ROPE_DIM // 2
    inv_freq = ROPE_THETA ** (-(jnp.arange(half, dtype=F32) * 2.0 / ROPE_DIM))
    ang = pos.astype(F32)[:, None] * inv_freq[None, :]
    cos, sin = jnp.cos(ang), jnp.sin(ang)
    n = pos.shape[0]
    ones = jnp.ones((n, HEAD_DIM - ROPE_DIM), F32)
    zeros = jnp.zeros((n, HEAD_DIM - ROPE_DIM), F32)
    zh = jnp.zeros((n, half), F32)
    c = jnp.concatenate([cos, cos, ones], axis=1)
    sa = jnp.concatenate([-sin, zh, zeros], axis=1)
    sb = jnp.concatenate([zh, sin, zeros], axis=1)
    reps = max(rows // n, 1)
    return tuple(jnp.tile(jnp.concatenate([a, a], axis=1), (reps, 1)) for a in (c, sa, sb))


def _block_diag(blocks):
    n, r, c = blocks.shape
    eye = jnp.eye(n, dtype=blocks.dtype)
    return (blocks[:, :, None, :] * eye[:, None, :, None]).reshape(n * r, n * c)


def _group_mean_matrix(width):
    g = jnp.arange(width) // HEAD_DIM
    return jnp.where(g[:, None] == g[None, :], 1.0 / HEAD_DIM, 0.0).astype(BF16)


def _pad_rows(a, rows):
    return jnp.pad(a, ((0, 0), (0, rows - a.shape[1]), (0, 0)))


def _layer_weights(l, norm_mix, w_in, conv_w, conv_b, rg_wa, rg_ba, rg_wx, rg_bx, rg_lambda, q_norm, k_norm,
                   gla_wa2, gla_ba, gla_norm, w_out, norm_mlp, w_up, w_down):
    wi = w_in[l]
    w_perm = jnp.concatenate(
        [wi[:, 0:1792], wi[:, 1860:2628], wi[:, 1792:1856], wi[:, 2628:2644], wi[:, 1856:1860],
         jnp.zeros((D_MODEL, N_PERM - 2644), wi.dtype)], axis=1).astype(BF16)
    wa2p = jnp.zeros((LANES, C_KW), F32).at[MISC_CA:MISC_CA + GATE_RANK].set(gla_wa2[l]).astype(BF16)
    return dict(
        g_mix=norm_mix[l][None, :], w_perm=w_perm,
        qn=jnp.tile(q_norm[l], B_HEADS)[None, :], kn=jnp.tile(k_norm[l], B_KV_HEADS)[None, :],
        cw8=jnp.pad(conv_w[l], ((0, SUBLANES - CONV_W), (0, 0))), cb=conv_b[l][None, :],
        wa=_block_diag(rg_wa[l]).astype(BF16), ba=rg_ba[l][None, :],
        wx=_block_diag(rg_wx[l]).astype(BF16), bx=rg_bx[l][None, :], lam=rg_lambda[l][None, :],
        wa2p=wa2p, gba=gla_ba[l][None, :], gn=jnp.tile(gla_norm[l], C_HEADS)[None, :],
        wo=w_out[l].astype(BF16), g_mlp=norm_mlp[l][None, :], wu=w_up[l].astype(BF16), wd=w_down[l].astype(BF16),
    )


def _state_to_blockdiag(s):
    nb = s.shape[0]
    eye = jnp.eye(C_HEADS, dtype=s.dtype)
    st = jnp.swapaxes(s, 2, 3)[:, :, :, None, :] * eye[None, :, None, :, None]
    return st.reshape(nb, C_WIDTH, C_KW)


def _blockdiag_to_state(st):
    nb = st.shape[0]
    st = st.reshape(nb, C_HEADS, C_DV, C_HEADS, C_DK)
    s = jnp.stack([st[:, h, :, h, :] for h in range(C_HEADS)], axis=1)
    return jnp.swapaxes(s, 2, 3)


def _trunk_layer(x, nb, t, pos, conv_state, h0, s0, lw, consts, attn_fn, tm, tc, ch, nch):
    bdq, bdk, bdv = consts
    m = nb * t
    tabs = _rope_tables(pos, tm)
    oa, oq, ok, okb, ov, ovb, oqi, omisc, oki2, oc = _inproj(
        x, lw["g_mix"], lw["w_perm"], tabs, lw["qn"], lw["kn"], bdq, bdk, tm)

    cs8 = jnp.pad(conv_state, ((0, 0), (SUBLANES - (CONV_W - 1), 0), (0, 0)))
    h08 = jnp.broadcast_to(h0[:, None, :], (nb, SUBLANES, A_WIDTH))
    ya, cnew, hlast = _rglru(oa, cs8, h08, lw["cw8"], lw["cb"], lw["wa"], lw["ba"], lw["wx"], lw["bx"],
                             lw["lam"], nb, t, tc)

    yc, st = _gla(oc, omisc, _state_to_blockdiag(s0), lw["wa2p"], lw["gba"], lw["gn"], bdv, nb, t, ch, nch)

    wi = omisc[:, MISC_WI:MISC_WI + IDX_HEADS]
    yb = attn_fn(oq, okb, ovb, oqi, oki2, wi)

    y = _outmlp(x, ya, yb, yc, lw["wo"], lw["g_mlp"], lw["wu"], lw["wd"], min(tm, MLP_ROWS))
    states = (ok.reshape(nb, t, B_KV_HEADS, HEAD_DIM), ov.reshape(nb, t, B_KV_HEADS, HEAD_DIM),
              omisc[:, :IDX_DIM].reshape(nb, t, IDX_DIM), cnew[:, SUBLANES - (CONV_W - 1):],
              hlast[:, SUBLANES - 1], _blockdiag_to_state(st))
    return y, states


def kernel(x_prompt, x_sample, cache_k, cache_v, cache_ki, state_conv, state_rglru, state_gla, page_table,
           norm_mix, w_in, conv_w, conv_b, rg_wa, rg_ba, rg_wx, rg_bx, rg_lambda, q_norm, k_norm,
           gla_wa2, gla_ba, gla_norm, w_out, norm_mlp, w_up, w_down):
    bp, tp, _ = x_prompt.shape
    bs, ts, _ = x_sample.shape
    depth = w_in.shape[0]
    npg = page_table.shape[1]
    past = npg * PAGE_SIZE
    assert tp % 512 == 0 and ts == SUBLANES and tp >= CONV_W - 1
    n_pool = cache_k.shape[1]
    cache_kt = jnp.transpose(cache_k, (0, 1, 3, 4, 2)).reshape(depth, n_pool, KV_WIDTH, PAGE_SIZE)
    cache_vt = jnp.transpose(cache_v, (0, 1, 3, 4, 2)).reshape(depth, n_pool, KV_WIDTH, PAGE_SIZE)
    cache_kit = jnp.transpose(cache_ki, (0, 1, 3, 2))
    consts = (_group_mean_matrix(B_WIDTH), _group_mean_matrix(KV_WIDTH), _group_mean_matrix(C_WIDTH))
    pos_p = jnp.arange(tp)
    pos_s = past + jnp.arange(ts)
    dt = x_prompt.dtype
    conv0 = jnp.zeros((bp, CONV_W - 1, A_WIDTH), dt)
    h00 = jnp.zeros((bp, A_WIDTH), dt)
    s00 = jnp.zeros((bp, C_HEADS, C_DK, C_DV), dt)

    yp = x_prompt.reshape(bp * tp, D_MODEL)
    ys = x_sample.reshape(bs * ts, D_MODEL)
    sp, ss = [], []
    for l in range(depth):
        lw = _layer_weights(l, norm_mix, w_in, conv_w, conv_b, rg_wa, rg_ba, rg_wx, rg_bx, rg_lambda, q_norm,
                            k_norm, gla_wa2, gla_ba, gla_norm, w_out, norm_mlp, w_up, w_down)

        def attn_p(oq, okb, ovb, oqi, oki2, wi):
            r3 = lambda a: a.reshape(bp, tp, a.shape[-1])
            return _dsa_prompt(r3(oqi), r3(wi), r3(oq), r3(oki2), r3(okb), r3(ovb), bp, tp).reshape(bp * tp, B_WIDTH)

        def attn_s(oq, okb, ovb, oqi, oki2, wi, l=l):
            r3 = lambda a: a.reshape(bs, ts, a.shape[-1])
            qi3, wi3 = r3(oqi), r3(wi)
            scores = _page_scores(page_table, qi3, wi3, cache_kit, l, math.gcd(npg, SCORE_PAGES))
            kin_pad = _pad_rows(r3(oki2)[:, :, :IDX_DIM], KEY_BLOCK)
            acc = _dsa_sample(page_table, scores, qi3, wi3, kin_pad, r3(oq), _pad_rows(r3(okb), KEY_BLOCK),
                              _pad_rows(r3(ovb), KEY_BLOCK), cache_kt, cache_vt, l, math.gcd(npg, ATTN_PAGES))
            acc = acc.reshape(bs, B_HEADS, ts, B_KV_HEADS, HEAD_DIM)
            o = jnp.stack([acc[:, h, :, h // 2, :] for h in range(B_HEADS)], axis=2)
            return o.reshape(bs * ts, B_WIDTH).astype(BF16)

        yp, st_p = _trunk_layer(yp, bp, tp, pos_p, conv0, h00, s00, lw, consts, attn_p,
                                tm=512, tc=256, ch=math.gcd(tp, CHUNK), nch=512 // math.gcd(tp, CHUNK))
        ys, st_s = _trunk_layer(ys, bs, ts, pos_s, state_conv[l], state_rglru[l], state_gla[l], lw, consts, attn_s,
                                tm=bs * ts, tc=ts, ch=math.gcd(ts, CHUNK), nch=1)
        sp.append(st_p)
        ss.append(st_s)

    outs = [yp.reshape(bp, tp, D_MODEL), ys.reshape(bs, ts, D_MODEL)]
    for states in (sp, ss):
        for idx in range(6):
            outs.append(jnp.stack([s[idx] for s in states]))
    return tuple(outs)
```
